```python
import jax, jax.numpy as jnp
from jax import lax
import numpy as np

D_MODEL = 1024
BATCH = 16
SEQ = 2048
DEPTH = 1

HEAD_DIM = 64
N_Q_HEADS = 8
N_KV_HEADS = 2
Q_PER_KV = N_Q_HEADS // N_KV_HEADS
ATTN_WIDTH = N_Q_HEADS * HEAD_DIM
KV_WIDTH = N_KV_HEADS * HEAD_DIM
WINDOW = 128
ATT_BLOCK = 128
CHUNK = 128
N_SG_GROUPS = 8
SG_GROUP_DIM = 64
SG_WIDTH = N_SG_GROUPS * SG_GROUP_DIM
SPLIT_SIZES = (ATTN_WIDTH, KV_WIDTH, KV_WIDTH, SG_WIDTH, SG_WIDTH, D_MODEL, D_MODEL)
IN_WIDTH = sum(SPLIT_SIZES)
SPLIT_POINTS = tuple(int(v) for v in np.cumsum(SPLIT_SIZES)[:-1])
N_EXPERTS = 32
TOP_K = 4
D_EXPERT = D_MODEL
SWIGLU_LIMIT = 7.0
SWIGLU_ALPHA = 1.702
MOE_BLOCK = 256
NORM_EPS = 1e-6

kernel_name = 'hybrid_gmlp_swa_sink_moe_block'


def rmsnorm(x, g):
    xf = x.astype(jnp.float32)
    xf = xf * lax.rsqrt(jnp.mean(xf * xf, axis=-1, keepdims=True) + NORM_EPS)
    return (xf * g.astype(jnp.float32)).astype(x.dtype)


def layernorm(x, g, b):
    xf = x.astype(jnp.float32)
    mu = jnp.mean(xf, axis=-1, keepdims=True)
    var = jnp.mean(jnp.square(xf - mu), axis=-1, keepdims=True)
    y = (xf - mu) * lax.rsqrt(var + NORM_EPS) * g.astype(jnp.float32) + b.astype(jnp.float32)
    return y.astype(x.dtype)


def alibi_slopes(n_heads):
    return jnp.asarray(2.0 ** (-8.0 * np.arange(1, n_heads + 1) / n_heads), dtype=jnp.float32)


def sliding_window_attention(q, k, v, sinks):
    B, S, _ = q.shape
    nb = S // ATT_BLOCK
    q = q.reshape(B, nb, ATT_BLOCK, N_KV_HEADS, Q_PER_KV, HEAD_DIM)
    k = k.reshape(B, nb, ATT_BLOCK, N_KV_HEADS, HEAD_DIM)
    v = v.reshape(B, nb, ATT_BLOCK, N_KV_HEADS, HEAD_DIM)

    def with_prev(t):
        prev = jnp.pad(t[:, :-1], ((0, 0), (1, 0), (0, 0), (0, 0), (0, 0)))
        return jnp.concatenate([prev, t], axis=2)

    kb, vb = with_prev(k), with_prev(v)
    scores = jnp.einsum('bnqhgd,bnkhd->bnhgqk', q, kb,
                        preferred_element_type=jnp.float32) * (HEAD_DIM ** -0.5)
    qi = jnp.arange(ATT_BLOCK)[:, None]
    kj = jnp.arange(2 * ATT_BLOCK)[None, :]
    dist = qi + ATT_BLOCK - kj
    in_win = (dist >= 0) & (dist < WINDOW)
    blk = jnp.arange(nb)[:, None, None]
    mask = in_win[None] & ((blk > 0) | (kj[None] >= ATT_BLOCK))
    slopes = alibi_slopes(N_Q_HEADS).reshape(N_KV_HEADS, Q_PER_KV)
    logits = scores - slopes[:, :, None, None] * dist.astype(jnp.float32)
    logits = jnp.where(mask[None, :, None, None], logits, -jnp.inf)
    sink = sinks.astype(jnp.float32).reshape(N_KV_HEADS, Q_PER_KV)[None, None, :, :, None, None]
    m = jnp.maximum(jnp.max(logits, axis=-1, keepdims=True), sink)
    p = jnp.exp(logits - m)
    probs = p / (jnp.sum(p, axis=-1, keepdims=True) + jnp.exp(sink - m))
    out = jnp.einsum('bnhgqk,bnkhd->bnqhgd', probs.astype(v.dtype), vb)
    return out.reshape(B, S, ATTN_WIDTH)


def chunked_spatial_gating(u, v, ln_g, ln_b, w_s, b_s):
    B, S, _ = u.shape
    nc = S // CHUNK
    vn = layernorm(v, ln_g, ln_b).reshape(B, nc, CHUNK, N_SG_GROUPS, SG_GROUP_DIM)
    causal = jnp.tril(jnp.ones((CHUNK, CHUNK), dtype=bool))
    w = jnp.where(causal[None], w_s, 0.0).astype(vn.dtype)
    mixed = jnp.einsum('gts,bcsgd->bctgd', w, vn) + b_s.T.astype(vn.dtype)[None, None, :, :, None]
    return u * mixed.reshape(B, S, SG_WIDTH)


def moe_ffn(h, w_router, b_router, w1, b1, w2, b2):
    B, S, D = h.shape
    T = B * S
    xt = h.reshape(T, D)
    logits = (xt @ w_router).astype(jnp.float32) + b_router.astype(jnp.float32)
    top_vals, top_idx = lax.top_k(logits, TOP_K)
    gates = jax.nn.softmax(top_vals, axis=-1)
    A = T * TOP_K
    flat_e = top_idx.reshape(A)
    flat_tok = jnp.arange(A, dtype=jnp.int32) // TOP_K
    flat_w = gates.reshape(A)
    order = jnp.argsort(flat_e, stable=True)
    se, stok, sw = flat_e[order], flat_tok[order], flat_w[order]
    counts = jnp.bincount(flat_e, length=N_EXPERTS)
    padded = (counts + MOE_BLOCK - 1) // MOE_BLOCK * MOE_BLOCK
    starts = jnp.cumsum(counts) - counts
    pends = jnp.cumsum(padded)
    pstarts = pends - padded
    slot = pstarts[se] + (jnp.arange(A, dtype=jnp.int32) - starts[se])
    n_blocks = -(-A // MOE_BLOCK) + N_EXPERTS
    P = n_blocks * MOE_BLOCK
    slot_tok = jnp.full((P,), T, dtype=jnp.int32).at[slot].set(stok)
    slot_w = jnp.zeros((P,), dtype=jnp.float32).at[slot].set(sw)
    block_start = jnp.arange(n_blocks, dtype=pends.dtype) * MOE_BLOCK
    block_e = jnp.minimum(jnp.searchsorted(pends, block_start, side='right'), N_EXPERTS - 1)
    x_pad = jnp.concatenate([xt, jnp.zeros((1, D), xt.dtype)], axis=0)
    xs = x_pad[slot_tok].reshape(n_blocks, MOE_BLOCK, D)

    def expert_block(args):
        xb, e = args
        hb = xb @ w1[e] + b1[e]
        g, lin = hb[:, :D_EXPERT], hb[:, D_EXPERT:]
        g = jnp.minimum(g, SWIGLU_LIMIT)
        lin = jnp.clip(lin, -SWIGLU_LIMIT, SWIGLU_LIMIT)
        act = g * jax.nn.sigmoid(SWIGLU_ALPHA * g) * (lin + 1.0)
        return act @ w2[e] + b2[e]

    ys = lax.map(expert_block, (xs, block_e)).reshape(P, D)
    ys = ys * slot_w[:, None].astype(ys.dtype)
    out = jax.ops.segment_sum(ys, slot_tok, num_segments=T + 1)[:T]
    return out.reshape(B, S, D)


def setup_inputs(seed: int = 0) -> dict:
    key = jax.random.key(seed)
    ks = jax.random.split(key, 28)
    L, D, E, I = DEPTH, D_MODEL, N_EXPERTS, D_EXPERT
    nrm = lambda k, shape, s: jax.random.normal(k, shape, jnp.float32) * s
    gain = lambda k, shape: 1.0 + 0.05 * jax.random.normal(k, shape, jnp.float32)
    return {
        'x': nrm(ks[0], (BATCH, SEQ, D), 1.0),
        'c': nrm(ks[1], (BATCH, D), 1.0),
        'w_mod': nrm(ks[2], (L, D, 6 * D), D ** -0.5),
        'b_mod': nrm(ks[3], (L, 6 * D), 0.02),
        'g_pre_mix': gain(ks[4], (L, D)),
        'g_post_mix': gain(ks[5], (L, D)),
        'w_in': nrm(ks[6], (L, D, IN_WIDTH), D ** -0.5),
        'b_in': nrm(ks[7], (L, IN_WIDTH), 0.02),
        'attn_sinks': nrm(ks[8], (L, N_Q_HEADS), 0.5),
        'sg_ln_g': gain(ks[9], (L, SG_WIDTH)),
        'sg_ln_b': nrm(ks[10], (L, SG_WIDTH), 0.02),
        'sg_w': nrm(ks[11], (L, N_SG_GROUPS, CHUNK, CHUNK), CHUNK ** -0.5),
        'sg_b': gain(ks[12], (L, N_SG_GROUPS, CHUNK)),
        'w_br_attn': nrm(ks[13], (L, ATTN_WIDTH, D), ATTN_WIDTH ** -0.5),
        'w_br_sg': nrm(ks[14], (L, SG_WIDTH, D), SG_WIDTH ** -0.5),
        'w_out': nrm(ks[15], (L, D, D), D ** -0.5),
        'b_out': nrm(ks[16], (L, D), 0.02),
        'g_pre_ffn': gain(ks[17], (L, D)),
        'g_post_ffn': gain(ks[18], (L, D)),
        'w_router': nrm(ks[19], (L, D, E), D ** -0.5),
        'b_router': nrm(ks[20], (L, E), 0.01),
        'w_mlp1': nrm(ks[21], (L, E, D, 2 * I), D ** -0.5),
        'b_mlp1': nrm(ks[22], (L, E, 2 * I), 0.02),
        'w_mlp2': nrm(ks[23], (L, E, I, D), I ** -0.5),
        'b_mlp2': nrm(ks[24], (L, E, D), 0.02),
    }


def reference(x, c, w_mod, b_mod, g_pre_mix, g_post_mix, w_in, b_in, attn_sinks,
              sg_ln_g, sg_ln_b, sg_w, sg_b, w_br_attn, w_br_sg, w_out, b_out,
              g_pre_ffn, g_post_ffn, w_router, b_router, w_mlp1, b_mlp1, w_mlp2, b_mlp2):
    c_act = jax.nn.silu(c)
    for l in range(DEPTH):
        mod = (c_act @ w_mod[l] + b_mod[l])[:, None, :]
        shift1, scale1, gate1, shift2, scale2, gate2 = jnp.split(mod, 6, axis=-1)

        h = rmsnorm(x, g_pre_mix[l]) * (1.0 + scale1) + shift1
        z = h @ w_in[l] + b_in[l]
        q, k, v, u_sg, v_sg, g_att, g_sg = jnp.split(z, SPLIT_POINTS, axis=-1)
        y_att = sliding_window_attention(q, k, v, attn_sinks[l])
        y_sg = chunked_spatial_gating(jax.nn.gelu(u_sg), jax.nn.gelu(v_sg),
                                      sg_ln_g[l], sg_ln_b[l], sg_w[l], sg_b[l])
        merged = (jax.nn.sigmoid(g_att) * (y_att @ w_br_attn[l])
                  + jax.nn.sigmoid(g_sg) * (y_sg @ w_br_sg[l]))
        y = merged @ w_out[l] + b_out[l]
        x = x + gate1 * rmsnorm(y, g_post_mix[l])

        h = rmsnorm(x, g_pre_ffn[l]) * (1.0 + scale2) + shift2
        y = moe_ffn(h, w_router[l], b_router[l], w_mlp1[l], b_mlp1[l], w_mlp2[l], b_mlp2[l])
        x = x + gate2 * rmsnorm(y, g_post_ffn[l])
    return x
```

```python
import functools

import jax
import jax.numpy as jnp
import numpy as np
from jax import lax
from jax.experimental import pallas as pl
from jax.experimental.pallas import tpu as pltpu

D_MODEL = 1024
BATCH = 16
SEQ = 2048
N_TOKENS = BATCH * SEQ
HEAD_DIM = 64
N_Q_HEADS = 8
N_KV_HEADS = 2
Q_PER_KV = N_Q_HEADS // N_KV_HEADS
ATTN_WIDTH = N_Q_HEADS * HEAD_DIM
KV_WIDTH = N_KV_HEADS * HEAD_DIM
ATT_BLOCK = 128
N_SG_GROUPS = 8
SG_GROUP_DIM = 64
SG_WIDTH = N_SG_GROUPS * SG_GROUP_DIM
N_EXPERTS = 32
TOP_K = 4
D_EXPERT = D_MODEL
SWIGLU_LIMIT = 7.0
SWIGLU_ALPHA = 1.702
MOE_BLOCK = 256
NORM_EPS = 1e-6
N_ASSIGN = N_TOKENS * TOP_K
N_MOE_BLOCKS = N_ASSIGN // MOE_BLOCK + N_EXPERTS
N_SLOTS = N_MOE_BLOCKS * MOE_BLOCK

_Q0 = 0
_K0 = _Q0 + ATTN_WIDTH
_V0 = _K0 + KV_WIDTH
_U0 = _V0 + KV_WIDTH
_SV0 = _U0 + SG_WIDTH
_GA0 = _SV0 + SG_WIDTH
_GS0 = _GA0 + D_MODEL
IN_WIDTH = _GS0 + D_MODEL

LANES = 128
SUBLANES = 8
ROW_TILES = D_MODEL // LANES
assert ROW_TILES == SUBLANES
VMEM_LIMIT_BYTES = 56 * 1024 * 1024

MIX_TILE = 256
CMB_TILE = 128
DMA_UNROLL = 8

_NEG_INF = float("-inf")


def _rms(x, g):
    return x * lax.rsqrt(jnp.mean(x * x, axis=-1, keepdims=True) + NORM_EPS) * g


def _bf16(x):
    return x.astype(jnp.bfloat16)


def _dot(a, b):
    return jnp.dot(a, b, preferred_element_type=jnp.float32)


def _mod_kernel(c_ref, w_ref, b_ref, o_ref):
    c = c_ref[...]
    c_act = c * jax.nn.sigmoid(c)
    o_ref[...] = _dot(_bf16(c_act), _bf16(w_ref[...])) + b_ref[...]


def _modulation(c, w_mod, b_mod):
    n_out = w_mod.shape[1]
    tn = D_MODEL
    return pl.pallas_call(
        _mod_kernel,
        grid=(n_out // tn,),
        in_specs=[
            pl.BlockSpec((BATCH, D_MODEL), lambda n: (0, 0)),
            pl.BlockSpec((D_MODEL, tn), lambda n: (0, n)),
            pl.BlockSpec((1, tn), lambda n: (0, n)),
        ],
        out_specs=pl.BlockSpec((BATCH, tn), lambda n: (0, n)),
        out_shape=jax.ShapeDtypeStruct((BATCH, n_out), jnp.float32),
        compiler_params=pltpu.CompilerParams(dimension_semantics=("arbitrary",)),
        name="modulation",
    )(c, w_mod, b_mod.reshape(1, n_out))


def _attention_bias():
    slopes = 2.0 ** (-8.0 * np.arange(1, N_Q_HEADS + 1) / N_Q_HEADS)
    qi = np.arange(ATT_BLOCK)[:, None]
    kj = np.arange(2 * ATT_BLOCK)[None, :]
    dist = qi + ATT_BLOCK - kj
    in_win = (dist >= 0) & (dist < ATT_BLOCK)
    bias = np.where(in_win[None], -slopes[:, None, None] * dist[None].astype(np.float64), -np.inf)
    bias = bias.reshape(N_KV_HEADS, Q_PER_KV * ATT_BLOCK, 2 * ATT_BLOCK)
    return jnp.asarray(bias, dtype=jnp.float32)


def _mixer_kernel(sinks_ref, x_ref, mod_ref, gpre_ref, gpost_ref, win_ref, bin_ref, abias_ref,
                  lng_ref, lnb_ref, sgw_ref, sgb_ref, wba_ref, wbs_ref, wout_ref, bout_ref,
                  gffn_ref, wr_ref, br_ref,
                  x1_ref, h2_ref, ri_ref, rg_ref, cnt_ref,
                  kprev_ref, vprev_ref, carry_ref):
    b = pl.program_id(0)
    j = pl.program_id(1)
    tm = MIX_TILE
    n_blk = tm // ATT_BLOCK

    @pl.when((b == 0) & (j == 0))
    def _():
        carry_ref[...] = jnp.zeros_like(carry_ref)

    @pl.when(j == 0)
    def _():
        kprev_ref[...] = jnp.zeros_like(kprev_ref)
        vprev_ref[...] = jnp.zeros_like(vprev_ref)

    x = x_ref[0]
    mod = mod_ref[0]
    shift1, scale1, gate1, shift2, scale2, gate2 = (
        mod[:, i * D_MODEL:(i + 1) * D_MODEL] for i in range(6))

    h = _bf16(_rms(x, gpre_ref[...]) * (1.0 + scale1) + shift1)

    def proj(lo, width):
        return _dot(h, win_ref[:, lo:lo + width]) + bin_ref[:, lo:lo + width]

    q = proj(_Q0, ATTN_WIDTH)
    k = proj(_K0, KV_WIDTH)
    v = proj(_V0, KV_WIDTH)
    row_group = lax.broadcasted_iota(jnp.int32, (Q_PER_KV * ATT_BLOCK, 1), 0) // ATT_BLOCK
    col = lax.broadcasted_iota(jnp.int32, (Q_PER_KV * ATT_BLOCK, 2 * ATT_BLOCK), 1)
    y_att_blocks = []
    for i in range(n_blk):
        rows = slice(i * ATT_BLOCK, (i + 1) * ATT_BLOCK)
        if i == 0:
            k_prev, v_prev = kprev_ref[...], vprev_ref[...]
        else:
            prev = slice((i - 1) * ATT_BLOCK, i * ATT_BLOCK)
            k_prev, v_prev = k[prev], v[prev]
        k_cat = _bf16(jnp.concatenate([k_prev, k[rows]], axis=0))
        v_cat = _bf16(jnp.concatenate([v_prev, v[rows]], axis=0))
        q_blk = _bf16(q[rows])
        heads = [None] * N_Q_HEADS
        for hk in range(N_KV_HEADS):
            kv_cols = slice(hk * HEAD_DIM, (hk + 1) * HEAD_DIM)
            q_stack = jnp.concatenate(
                [q_blk[:, (hk * Q_PER_KV + g) * HEAD_DIM:(hk * Q_PER_KV + g + 1) * HEAD_DIM]
                 for g in range(Q_PER_KV)], axis=0)
            scores = lax.dot_general(q_stack, k_cat[:, kv_cols], (((1,), (1,)), ((), ())),
                                     preferred_element_type=jnp.float32)
            logits = scores * (HEAD_DIM ** -0.5) + abias_ref[hk]
            if i == 0:
                n_masked = jnp.where(j == 0, ATT_BLOCK, 0)
                logits = jnp.where(col < n_masked, _NEG_INF, logits)
            sink = jnp.zeros((Q_PER_KV * ATT_BLOCK, 1), jnp.float32)
            for g in range(Q_PER_KV):
                sink = jnp.where(row_group == g, sinks_ref[hk * Q_PER_KV + g], sink)
            m = jnp.maximum(jnp.max(logits, axis=-1, keepdims=True), sink)
            p = jnp.exp(logits - m)
            probs = p / (jnp.sum(p, axis=-1, keepdims=True) + jnp.exp(sink - m))
            out = _dot(_bf16(probs), v_cat[:, kv_cols])
            for g in range(Q_PER_KV):
                heads[hk * Q_PER_KV + g] = out[g * ATT_BLOCK:(g + 1) * ATT_BLOCK]
        y_att_blocks.append(jnp.concatenate(heads, axis=1))
    kprev_ref[...] = k[tm - ATT_BLOCK:]
    vprev_ref[...] = v[tm - ATT_BLOCK:]
    y_att = jnp.concatenate(y_att_blocks, axis=0)
    branch_att = _dot(_bf16(y_att), wba_ref[...])

    u = jax.nn.gelu(proj(_U0, SG_WIDTH))
    sv = jax.nn.gelu(proj(_SV0, SG_WIDTH))
    mu = jnp.mean(sv, axis=-1, keepdims=True)
    var = jnp.mean(jnp.square(sv - mu), axis=-1, keepdims=True)
    vn = _bf16((sv - mu) * lax.rsqrt(var + NORM_EPS) * lng_ref[...] + lnb_ref[...])
    causal = (lax.broadcasted_iota(jnp.int32, (ATT_BLOCK, ATT_BLOCK), 0)
              >= lax.broadcasted_iota(jnp.int32, (ATT_BLOCK, ATT_BLOCK), 1))
    w_sp = [_bf16(jnp.where(causal, sgw_ref[g], 0.0)) for g in range(N_SG_GROUPS)]
    mixed_chunks = []
    for cidx in range(n_blk):
        rows = slice(cidx * ATT_BLOCK, (cidx + 1) * ATT_BLOCK)
        groups = [_dot(w_sp[g], vn[rows, g * SG_GROUP_DIM:(g + 1) * SG_GROUP_DIM])
                  for g in range(N_SG_GROUPS)]
        mixed_chunks.append(jnp.concatenate(groups, axis=1) + sgb_ref[...])
    y_sg = u * jnp.concatenate(mixed_chunks, axis=0)
    branch_sg = _dot(_bf16(y_sg), wbs_ref[...])

    merged = (jax.nn.sigmoid(proj(_GA0, D_MODEL)) * branch_att
              + jax.nn.sigmoid(proj(_GS0, D_MODEL)) * branch_sg)
    y = _dot(_bf16(merged), wout_ref[...]) + bout_ref[...]
    x1 = x + gate1 * _rms(y, gpost_ref[...])
    x1_ref[0] = x1

    h2 = _rms(x1, gffn_ref[...]) * (1.0 + scale2) + shift2
    for s in range(ROW_TILES):
        h2_ref[pl.ds(s, tm, stride=ROW_TILES), :] = h2[:, s * LANES:(s + 1) * LANES]
    logits = _dot(_bf16(h2), wr_ref[...]) + br_ref[...]
    lane = lax.broadcasted_iota(jnp.int32, (tm, LANES), 1).astype(jnp.float32)
    top_val, top_idx = [], []
    selected = jnp.zeros((tm, LANES), jnp.float32)
    for _ in range(TOP_K):
        mval = jnp.max(logits, axis=-1, keepdims=True)
        midx = jnp.min(jnp.where(logits == mval, lane, float(LANES)), axis=-1, keepdims=True)
        hit = lane == midx
        selected = jnp.where(hit, 1.0, selected)
        logits = jnp.where(hit, _NEG_INF, logits)
        top_val.append(mval)
        top_idx.append(midx)
    expv = [jnp.exp(tv - top_val[0]) for tv in top_val]
    denom = expv[0] + expv[1] + expv[2] + expv[3]
    strict_lower = (lax.broadcasted_iota(jnp.int32, (tm, tm), 0)
                    > lax.broadcasted_iota(jnp.int32, (tm, tm), 1))
    before = _dot(_bf16(jnp.where(strict_lower, 1.0, 0.0)), _bf16(selected)) + carry_ref[0:1, :]
    route_i = jnp.zeros((tm, LANES), jnp.float32)
    route_g = jnp.zeros((tm, LANES), jnp.float32)
    for kk in range(TOP_K):
        rank = jnp.sum(jnp.where(lane == top_idx[kk], before, 0.0), axis=-1, keepdims=True)
        route_i = jnp.where(lane == float(kk), top_idx[kk], route_i)
        route_i = jnp.where(lane == float(TOP_K + kk), rank, route_i)
        route_g = jnp.where(lane == float(kk), expv[kk] / denom, route_g)
    ri_ref[...] = route_i.astype(jnp.int32)
    rg_ref[...] = route_g
    total = carry_ref[0:1, :] + jnp.sum(selected, axis=0, keepdims=True)
    carry_ref[...] = jnp.broadcast_to(total, carry_ref.shape)
    cnt_ref[...] = jnp.broadcast_to(total, cnt_ref.shape)


def _mixer(x, mod, p):
    tm = MIX_TILE
    n_j = SEQ // tm
    const2 = lambda shape: pl.BlockSpec(shape, lambda b, j, s: (0, 0), pipeline_mode=pl.Buffered(1))
    const3 = lambda shape: pl.BlockSpec(shape, lambda b, j, s: (0, 0, 0), pipeline_mode=pl.Buffered(1))
    tile_rows = lambda b, j, s: (b * n_j + j, 0)
    in_specs = [
        pl.BlockSpec((1, tm, D_MODEL), lambda b, j, s: (b, j, 0)),
        pl.BlockSpec((1, 1, 6 * D_MODEL), lambda b, j, s: (b, 0, 0)),
        const2((1, D_MODEL)), const2((1, D_MODEL)),
        const2((D_MODEL, IN_WIDTH)), const2((1, IN_WIDTH)),
        const3((N_KV_HEADS, Q_PER_KV * ATT_BLOCK, 2 * ATT_BLOCK)),
        const2((1, SG_WIDTH)), const2((1, SG_WIDTH)),
        const3((N_SG_GROUPS, ATT_BLOCK, ATT_BLOCK)), const2((ATT_BLOCK, SG_WIDTH)),
        const2((ATTN_WIDTH, D_MODEL)), const2((SG_WIDTH, D_MODEL)),
        const2((D_MODEL, D_MODEL)), const2((1, D_MODEL)),
        const2((1, D_MODEL)),
        const2((D_MODEL, LANES)), const2((1, LANES)),
    ]
    out_specs = [
        pl.BlockSpec((1, tm, D_MODEL), lambda b, j, s: (b, j, 0)),
        pl.BlockSpec((tm * ROW_TILES, LANES), tile_rows),
        pl.BlockSpec((tm, LANES), tile_rows),
        pl.BlockSpec((tm, LANES), tile_rows),
        pl.BlockSpec((SUBLANES, LANES), lambda b, j, s: (0, 0)),
    ]
    out_shape = [
        jax.ShapeDtypeStruct((BATCH, SEQ, D_MODEL), jnp.float32),
        jax.ShapeDtypeStruct((N_TOKENS * ROW_TILES, LANES), jnp.float32),
        jax.ShapeDtypeStruct((N_TOKENS, LANES), jnp.int32),
        jax.ShapeDtypeStruct((N_TOKENS, LANES), jnp.float32),
        jax.ShapeDtypeStruct((SUBLANES, LANES), jnp.float32),
    ]
    return pl.pallas_call(
        _mixer_kernel,
        grid_spec=pltpu.PrefetchScalarGridSpec(
            num_scalar_prefetch=1,
            grid=(BATCH, n_j),
            in_specs=in_specs,
            out_specs=out_specs,
            scratch_shapes=[
                pltpu.VMEM((ATT_BLOCK, KV_WIDTH), jnp.float32),
                pltpu.VMEM((ATT_BLOCK, KV_WIDTH), jnp.float32),
                pltpu.VMEM((SUBLANES, LANES), jnp.float32),
            ]),
        out_shape=out_shape,
        compiler_params=pltpu.CompilerParams(
            dimension_semantics=("arbitrary", "arbitrary"), vmem_limit_bytes=VMEM_LIMIT_BYTES),
        name="mixer_router",
    )(p["sinks"], x, mod.reshape(BATCH, 1, 6 * D_MODEL), p["g_pre_mix"], p["g_post_mix"],
      p["w_in"], p["b_in"], _attention_bias(), p["sg_ln_g"], p["sg_ln_b"], p["sg_w"], p["sg_b"],
      p["w_br_attn"], p["w_br_sg"], p["w_out"], p["b_out"], p["g_pre_ffn"],
      p["w_router"], p["b_router"])


def _gather_rows(idx_at, n_rows, src_hbm, dst_vmem, sem):
    def row_copy(i):
        src_row = pl.multiple_of(idx_at(i) * ROW_TILES, ROW_TILES)
        dst_row = pl.multiple_of(i * ROW_TILES, ROW_TILES)
        return pltpu.make_async_copy(src_hbm.at[pl.ds(src_row, ROW_TILES)],
                                     dst_vmem.at[pl.ds(dst_row, ROW_TILES)], sem)

    def issue(g, carry):
        for r in range(DMA_UNROLL):
            row_copy(g * DMA_UNROLL + r).start()
        return carry

    lax.fori_loop(0, n_rows // DMA_UNROLL, issue, 0)
    pltpu.make_async_copy(src_hbm.at[pl.ds(0, n_rows * ROW_TILES)],
                          dst_vmem.at[pl.ds(0, n_rows * ROW_TILES)], sem).wait()


def _rows_to_matrix(rows_ref, first_row, n_rows):
    return jnp.concatenate(
        [rows_ref[pl.ds(first_row * ROW_TILES + s, n_rows, stride=ROW_TILES), :]
         for s in range(ROW_TILES)], axis=1)


def _expert_kernel(be_ref, tok_ref, h2_hbm, w1_ref, b1_ref, w2_ref, b2_ref, ys_ref, xg_ref, sem):
    del be_ref
    _gather_rows(lambda i: tok_ref[0, 0, i], MOE_BLOCK, h2_hbm, xg_ref, sem)
    xb = _bf16(_rows_to_matrix(xg_ref, 0, MOE_BLOCK))
    hb = _dot(xb, w1_ref[0]) + b1_ref[0]
    g = jnp.minimum(hb[:, :D_EXPERT], SWIGLU_LIMIT)
    lin = jnp.clip(hb[:, D_EXPERT:], -SWIGLU_LIMIT, SWIGLU_LIMIT)
    act = g * jax.nn.sigmoid(SWIGLU_ALPHA * g) * (lin + 1.0)
    y = _dot(_bf16(act), w2_ref[0]) + b2_ref[0]
    for s in range(ROW_TILES):
        ys_ref[pl.ds(s, MOE_BLOCK, stride=ROW_TILES), :] = y[:, s * LANES:(s + 1) * LANES]


def _experts(block_e, slot_tok, h2_rows, w1, b1, w2, b2):
    return pl.pallas_call(
        _expert_kernel,
        grid_spec=pltpu.PrefetchScalarGridSpec(
            num_scalar_prefetch=1,
            grid=(N_MOE_BLOCKS,),
            in_specs=[
                pl.BlockSpec((1, 1, MOE_BLOCK), lambda i, be: (i, 0, 0), memory_space=pltpu.SMEM),
                pl.BlockSpec(memory_space=pl.ANY),
                pl.BlockSpec((1, D_MODEL, 2 * D_EXPERT), lambda i, be: (be[i], 0, 0)),
                pl.BlockSpec((1, 1, 2 * D_EXPERT), lambda i, be: (be[i], 0, 0)),
                pl.BlockSpec((1, D_EXPERT, D_MODEL), lambda i, be: (be[i], 0, 0)),
                pl.BlockSpec((1, 1, D_MODEL), lambda i, be: (be[i], 0, 0)),
            ],
            out_specs=pl.BlockSpec((MOE_BLOCK * ROW_TILES, LANES), lambda i, be: (i, 0)),
            scratch_shapes=[
                pltpu.VMEM((MOE_BLOCK * ROW_TILES, LANES), jnp.float32),
                pltpu.SemaphoreType.DMA,
            ]),
        out_shape=jax.ShapeDtypeStruct((N_SLOTS * ROW_TILES, LANES), jnp.float32),
        compiler_params=pltpu.CompilerParams(
            dimension_semantics=("arbitrary",), vmem_limit_bytes=VMEM_LIMIT_BYTES),
        name="experts",
    )(block_e, slot_tok.reshape(N_MOE_BLOCKS, 1, MOE_BLOCK), h2_rows,
      w1, b1.reshape(N_EXPERTS, 1, 2 * D_EXPERT), w2, b2.reshape(N_EXPERTS, 1, D_MODEL))


def _combine_kernel(slot_ref, ys_hbm, x1_ref, rg_ref, mod_ref, gpost_ref, o_ref, buf_ref, sem):
    tm = CMB_TILE
    _gather_rows(lambda i: slot_ref[0, 0, i], TOP_K * tm, ys_hbm, buf_ref, sem)
    gates = rg_ref[...]
    y = jnp.zeros((tm, D_MODEL), jnp.float32)
    for kk in range(TOP_K):
        y = y + _rows_to_matrix(buf_ref, kk * tm, tm) * gates[:, kk:kk + 1]
    gate2 = mod_ref[0][:, 5 * D_MODEL:6 * D_MODEL]
    o_ref[...] = x1_ref[...] + gate2 * _rms(y, gpost_ref[...])


def _combine(slot_kmajor, ys_rows, x1, route_g, mod, g_post_ffn):
    tm = CMB_TILE
    n_tiles = N_TOKENS // tm
    tiles_per_seq = SEQ // tm
    return pl.pallas_call(
        _combine_kernel,
        grid=(n_tiles,),
        in_specs=[
            pl.BlockSpec((1, 1, TOP_K * tm), lambda i: (i, 0, 0), memory_space=pltpu.SMEM),
            pl.BlockSpec(memory_space=pl.ANY),
            pl.BlockSpec((tm, D_MODEL), lambda i: (i, 0)),
            pl.BlockSpec((tm, LANES), lambda i: (i, 0)),
            pl.BlockSpec((1, 1, 6 * D_MODEL), lambda i: (i // tiles_per_seq, 0, 0)),
            pl.BlockSpec((1, D_MODEL), lambda i: (0, 0)),
        ],
        out_specs=pl.BlockSpec((tm, D_MODEL), lambda i: (i, 0)),
        out_shape=jax.ShapeDtypeStruct((N_TOKENS, D_MODEL), jnp.float32),
        scratch_shapes=[
            pltpu.VMEM((TOP_K * tm * ROW_TILES, LANES), jnp.float32),
            pltpu.SemaphoreType.DMA,
        ],
        compiler_params=pltpu.CompilerParams(
            dimension_semantics=("arbitrary",), vmem_limit_bytes=VMEM_LIMIT_BYTES),
        name="combine",
    )(slot_kmajor, ys_rows, x1, route_g, mod.reshape(BATCH, 1, 6 * D_MODEL), g_post_ffn)


def kernel(x, c, w_mod, b_mod, g_pre_mix, g_post_mix, w_in, b_in, attn_sinks, sg_ln_g, sg_ln_b,
           sg_w, sg_b, w_br_attn, w_br_sg, w_out, b_out, g_pre_ffn, g_post_ffn, w_router,
           b_router, w_mlp1, b_mlp1, w_mlp2, b_mlp2):
    assert x.shape == (BATCH, SEQ, D_MODEL) and w_mod.shape[0] == 1, "single-layer shapes only"
    row = lambda a: a[0].reshape(1, -1)
    bf = lambda a: a[0].astype(jnp.bfloat16)

    mod = _modulation(c, w_mod[0], b_mod[0])

    params = dict(
        sinks=attn_sinks[0], g_pre_mix=row(g_pre_mix), g_post_mix=row(g_post_mix),
        w_in=bf(w_in), b_in=row(b_in), sg_ln_g=row(sg_ln_g), sg_ln_b=row(sg_ln_b), sg_w=sg_w[0],
        sg_b=jnp.repeat(sg_b[0].T, SG_GROUP_DIM, axis=1),
        w_br_attn=bf(w_br_attn), w_br_sg=bf(w_br_sg), w_out=bf(w_out), b_out=row(b_out),
        g_pre_ffn=row(g_pre_ffn),
        w_router=jnp.pad(bf(w_router), ((0, 0), (0, LANES - N_EXPERTS))),
        b_router=jnp.pad(row(b_router), ((0, 0), (0, LANES - N_EXPERTS)), constant_values=_NEG_INF),
    )
    x1, h2_rows, route_i, route_g, counts = _mixer(x, mod, params)

    counts = counts[0, :N_EXPERTS].astype(jnp.int32)
    padded = (counts + MOE_BLOCK - 1) // MOE_BLOCK * MOE_BLOCK
    pends = jnp.cumsum(padded)
    pstarts = pends - padded
    block_start = jnp.arange(N_MOE_BLOCKS, dtype=jnp.int32) * MOE_BLOCK
    block_e = jnp.minimum(jnp.sum((pends[None, :] <= block_start[:, None]).astype(jnp.int32), axis=1),
                          N_EXPERTS - 1)
    top_idx = route_i[:, :TOP_K]
    slot = pstarts[top_idx] + route_i[:, TOP_K:2 * TOP_K]
    flat_tok = jnp.arange(N_ASSIGN, dtype=jnp.int32) // TOP_K
    slot_tok = jnp.zeros((N_SLOTS,), jnp.int32).at[slot.reshape(-1)].set(flat_tok)

    ys_rows = _experts(block_e, slot_tok, h2_rows, bf(w_mlp1), b_mlp1[0], bf(w_mlp2), b_mlp2[0])

    n_tiles = N_TOKENS // CMB_TILE
    slot_kmajor = slot.reshape(n_tiles, CMB_TILE, TOP_K).transpose(0, 2, 1).reshape(
        n_tiles, 1, TOP_K * CMB_TILE)
    out = _combine(slot_kmajor, ys_rows, x1.reshape(N_TOKENS, D_MODEL), route_g, mod, row(g_post_ffn))
    return out.reshape(BATCH, SEQ, D_MODEL)
```

```python
import functools

import jax
import jax.numpy as jnp
import numpy as np
from jax import lax
from jax.experimental import pallas as pl
from jax.experimental.pallas import tpu as pltpu

D_MODEL = 1024
BATCH = 16
SEQ = 2048
N_TOKENS = BATCH * SEQ
HEAD_DIM = 64
N_Q_HEADS = 8
N_KV_HEADS = 2
Q_PER_KV = N_Q_HEADS // N_KV_HEADS
ATTN_WIDTH = N_Q_HEADS * HEAD_DIM
KV_WIDTH = N_KV_HEADS * HEAD_DIM
ATT_BLOCK = 128
N_SG_GROUPS = 8
SG_GROUP_DIM = 64
SG_WIDTH = N_SG_GROUPS * SG_GROUP_DIM
N_EXPERTS = 32
TOP_K = 4
D_EXPERT = D_MODEL
SWIGLU_LIMIT = 7.0
SWIGLU_ALPHA = 1.702
MOE_BLOCK = 256
NORM_EPS = 1e-6
N_ASSIGN = N_TOKENS * TOP_K
N_MOE_BLOCKS = N_ASSIGN // MOE_BLOCK + N_EXPERTS
N_SLOTS = N_MOE_BLOCKS * MOE_BLOCK

_Q0 = 0
_K0 = _Q0 + ATTN_WIDTH
_V0 = _K0 + KV_WIDTH
_U0 = _V0 + KV_WIDTH
_SV0 = _U0 + SG_WIDTH
_GA0 = _SV0 + SG_WIDTH
_GS0 = _GA0 + D_MODEL
IN_WIDTH = _GS0 + D_MODEL

LANES = 128
SUBLANES = 8
ROW_TILES = D_MODEL // LANES
assert ROW_TILES == SUBLANES
VMEM_LIMIT_BYTES = 56 * 1024 * 1024

MIX_TILE = 256
CMB_TILE = 128
DSP_TILE = 256
DMA_UNROLL = 8
N_PAD_SLOTS = N_SLOTS - N_ASSIGN
DSP_PADS = N_PAD_SLOTS // (N_TOKENS // DSP_TILE)
assert DMA_UNROLL % TOP_K == 0 and DSP_PADS % DMA_UNROLL == 0

_NEG_INF = float("-inf")


def _rms(x, g):
    return x * lax.rsqrt(jnp.mean(x * x, axis=-1, keepdims=True) + NORM_EPS) * g


def _bf16(x):
    return x.astype(jnp.bfloat16)


def _dot(a, b):
    return jnp.dot(a, b, preferred_element_type=jnp.float32)


def _mod_kernel(c_ref, w_ref, b_ref, o_ref):
    c = c_ref[...]
    c_act = c * jax.nn.sigmoid(c)
    o_ref[...] = _dot(_bf16(c_act), _bf16(w_ref[...])) + b_ref[...]


def _modulation(c, w_mod, b_mod):
    n_out = w_mod.shape[1]
    tn = D_MODEL
    return pl.pallas_call(
        _mod_kernel,
        grid=(n_out // tn,),
        in_specs=[
            pl.BlockSpec((BATCH, D_MODEL), lambda n: (0, 0)),
            pl.BlockSpec((D_MODEL, tn), lambda n: (0, n)),
            pl.BlockSpec((1, tn), lambda n: (0, n)),
        ],
        out_specs=pl.BlockSpec((BATCH, tn), lambda n: (0, n)),
        out_shape=jax.ShapeDtypeStruct((BATCH, n_out), jnp.float32),
        compiler_params=pltpu.CompilerParams(dimension_semantics=("arbitrary",)),
        name="modulation",
    )(c, w_mod, b_mod.reshape(1, n_out))


def _attention_bias():
    slopes = 2.0 ** (-8.0 * np.arange(1, N_Q_HEADS + 1) / N_Q_HEADS)
    qi = np.arange(ATT_BLOCK)[:, None]
    kj = np.arange(2 * ATT_BLOCK)[None, :]
    dist = qi + ATT_BLOCK - kj
    in_win = (dist >= 0) & (dist < ATT_BLOCK)
    bias = np.where(in_win[None], -slopes[:, None, None] * dist[None].astype(np.float64), -np.inf)
    bias = bias.reshape(N_KV_HEADS, Q_PER_KV * ATT_BLOCK, 2 * ATT_BLOCK)
    return jnp.asarray(bias, dtype=jnp.float32)


def _mixer_kernel(sinks_ref, x_ref, mod_ref, gpre_ref, gpost_ref, win_ref, bin_ref, abias_ref,
                  lng_ref, lnb_ref, sgw_ref, sgb_ref, wba_ref, wbs_ref, wout_ref, bout_ref,
                  gffn_ref, wr_ref, br_ref,
                  x1_ref, h2_ref, ri_ref, rg_ref, cnt_ref,
                  kprev_ref, vprev_ref, carry_ref):
    b = pl.program_id(0)
    j = pl.program_id(1)
    tm = MIX_TILE
    n_blk = tm // ATT_BLOCK

    @pl.when((b == 0) & (j == 0))
    def _():
        carry_ref[...] = jnp.zeros_like(carry_ref)

    @pl.when(j == 0)
    def _():
        kprev_ref[...] = jnp.zeros_like(kprev_ref)
        vprev_ref[...] = jnp.zeros_like(vprev_ref)

    x = x_ref[0]
    mod = mod_ref[0]
    shift1, scale1, gate1, shift2, scale2, gate2 = (
        mod[:, i * D_MODEL:(i + 1) * D_MODEL] for i in range(6))

    h = _bf16(_rms(x, gpre_ref[...]) * (1.0 + scale1) + shift1)

    def proj(lo, width):
        return _dot(h, win_ref[:, lo:lo + width]) + bin_ref[:, lo:lo + width]

    q = proj(_Q0, ATTN_WIDTH)
    k = proj(_K0, KV_WIDTH)
    v = proj(_V0, KV_WIDTH)
    row_group = lax.broadcasted_iota(jnp.int32, (Q_PER_KV * ATT_BLOCK, 1), 0) // ATT_BLOCK
    col = lax.broadcasted_iota(jnp.int32, (Q_PER_KV * ATT_BLOCK, 2 * ATT_BLOCK), 1)
    y_att_blocks = []
    for i in range(n_blk):
        rows = slice(i * ATT_BLOCK, (i + 1) * ATT_BLOCK)
        if i == 0:
            k_prev, v_prev = kprev_ref[...], vprev_ref[...]
        else:
            prev = slice((i - 1) * ATT_BLOCK, i * ATT_BLOCK)
            k_prev, v_prev = k[prev], v[prev]
        k_cat = _bf16(jnp.concatenate([k_prev, k[rows]], axis=0))
        v_cat = _bf16(jnp.concatenate([v_prev, v[rows]], axis=0))
        q_blk = _bf16(q[rows])
        heads = [None] * N_Q_HEADS
        for hk in range(N_KV_HEADS):
            kv_cols = slice(hk * HEAD_DIM, (hk + 1) * HEAD_DIM)
            q_stack = jnp.concatenate(
                [q_blk[:, (hk * Q_PER_KV + g) * HEAD_DIM:(hk * Q_PER_KV + g + 1) * HEAD_DIM]
                 for g in range(Q_PER_KV)], axis=0)
            scores = lax.dot_general(q_stack, k_cat[:, kv_cols], (((1,), (1,)), ((), ())),
                                     preferred_element_type=jnp.float32)
            logits = scores * (HEAD_DIM ** -0.5) + abias_ref[hk]
            if i == 0:
                n_masked = jnp.where(j == 0, ATT_BLOCK, 0)
                logits = jnp.where(col < n_masked, _NEG_INF, logits)
            sink = jnp.zeros((Q_PER_KV * ATT_BLOCK, 1), jnp.float32)
            for g in range(Q_PER_KV):
                sink = jnp.where(row_group == g, sinks_ref[hk * Q_PER_KV + g], sink)
            m = jnp.maximum(jnp.max(logits, axis=-1, keepdims=True), sink)
            p = jnp.exp(logits - m)
            probs = p / (jnp.sum(p, axis=-1, keepdims=True) + jnp.exp(sink - m))
            out = _dot(_bf16(probs), v_cat[:, kv_cols])
            for g in range(Q_PER_KV):
                heads[hk * Q_PER_KV + g] = out[g * ATT_BLOCK:(g + 1) * ATT_BLOCK]
        y_att_blocks.append(jnp.concatenate(heads, axis=1))
    kprev_ref[...] = k[tm - ATT_BLOCK:]
    vprev_ref[...] = v[tm - ATT_BLOCK:]
    y_att = jnp.concatenate(y_att_blocks, axis=0)
    branch_att = _dot(_bf16(y_att), wba_ref[...])

    u = jax.nn.gelu(proj(_U0, SG_WIDTH))
    sv = jax.nn.gelu(proj(_SV0, SG_WIDTH))
    mu = jnp.mean(sv, axis=-1, keepdims=True)
    var = jnp.mean(jnp.square(sv - mu), axis=-1, keepdims=True)
    vn = _bf16((sv - mu) * lax.rsqrt(var + NORM_EPS) * lng_ref[...] + lnb_ref[...])
    causal = (lax.broadcasted_iota(jnp.int32, (ATT_BLOCK, ATT_BLOCK), 0)
              >= lax.broadcasted_iota(jnp.int32, (ATT_BLOCK, ATT_BLOCK), 1))
    w_sp = [_bf16(jnp.where(causal, sgw_ref[g], 0.0)) for g in range(N_SG_GROUPS)]
    mixed_chunks = []
    for cidx in range(n_blk):
        rows = slice(cidx * ATT_BLOCK, (cidx + 1) * ATT_BLOCK)
        groups = [_dot(w_sp[g], vn[rows, g * SG_GROUP_DIM:(g + 1) * SG_GROUP_DIM])
                  for g in range(N_SG_GROUPS)]
        mixed_chunks.append(jnp.concatenate(groups, axis=1) + sgb_ref[...])
    y_sg = u * jnp.concatenate(mixed_chunks, axis=0)
    branch_sg = _dot(_bf16(y_sg), wbs_ref[...])

    merged = (jax.nn.sigmoid(proj(_GA0, D_MODEL)) * branch_att
              + jax.nn.sigmoid(proj(_GS0, D_MODEL)) * branch_sg)
    y = _dot(_bf16(merged), wout_ref[...]) + bout_ref[...]
    x1 = x + gate1 * _rms(y, gpost_ref[...])
    x1_ref[0] = x1

    h2 = _rms(x1, gffn_ref[...]) * (1.0 + scale2) + shift2
    for s in range(ROW_TILES):
        h2_ref[pl.ds(s, tm, stride=ROW_TILES), :] = h2[:, s * LANES:(s + 1) * LANES]
    logits = _dot(_bf16(h2), wr_ref[...]) + br_ref[...]
    lane = lax.broadcasted_iota(jnp.int32, (tm, LANES), 1).astype(jnp.float32)
    top_val, top_idx = [], []
    selected = jnp.zeros((tm, LANES), jnp.float32)
    for _ in range(TOP_K):
        mval = jnp.max(logits, axis=-1, keepdims=True)
        midx = jnp.min(jnp.where(logits == mval, lane, float(LANES)), axis=-1, keepdims=True)
        hit = lane == midx
        selected = jnp.where(hit, 1.0, selected)
        logits = jnp.where(hit, _NEG_INF, logits)
        top_val.append(mval)
        top_idx.append(midx)
    expv = [jnp.exp(tv - top_val[0]) for tv in top_val]
    denom = expv[0] + expv[1] + expv[2] + expv[3]
    strict_lower = (lax.broadcasted_iota(jnp.int32, (tm, tm), 0)
                    > lax.broadcasted_iota(jnp.int32, (tm, tm), 1))
    before = _dot(_bf16(jnp.where(strict_lower, 1.0, 0.0)), _bf16(selected)) + carry_ref[0:1, :]
    route_i = jnp.zeros((tm, LANES), jnp.float32)
    route_g = jnp.zeros((tm, LANES), jnp.float32)
    for kk in range(TOP_K):
        rank = jnp.sum(jnp.where(lane == top_idx[kk], before, 0.0), axis=-1, keepdims=True)
        route_i = jnp.where(lane == float(kk), top_idx[kk], route_i)
        route_i = jnp.where(lane == float(TOP_K + kk), rank, route_i)
        route_g = jnp.where(lane == float(kk), expv[kk] / denom, route_g)
    ri_ref[...] = route_i.astype(jnp.int32)
    rg_ref[...] = route_g
    total = carry_ref[0:1, :] + jnp.sum(selected, axis=0, keepdims=True)
    carry_ref[...] = jnp.broadcast_to(total, carry_ref.shape)
    cnt_ref[...] = jnp.broadcast_to(total, cnt_ref.shape)


def _mixer(x, mod, p):
    tm = MIX_TILE
    n_j = SEQ // tm
    const2 = lambda shape: pl.BlockSpec(shape, lambda b, j, s: (0, 0), pipeline_mode=pl.Buffered(1))
    const3 = lambda shape: pl.BlockSpec(shape, lambda b, j, s: (0, 0, 0), pipeline_mode=pl.Buffered(1))
    tile_rows = lambda b, j, s: (b * n_j + j, 0)
    in_specs = [
        pl.BlockSpec((1, tm, D_MODEL), lambda b, j, s: (b, j, 0)),
        pl.BlockSpec((1, 1, 6 * D_MODEL), lambda b, j, s: (b, 0, 0)),
        const2((1, D_MODEL)), const2((1, D_MODEL)),
        const2((D_MODEL, IN_WIDTH)), const2((1, IN_WIDTH)),
        const3((N_KV_HEADS, Q_PER_KV * ATT_BLOCK, 2 * ATT_BLOCK)),
        const2((1, SG_WIDTH)), const2((1, SG_WIDTH)),
        const3((N_SG_GROUPS, ATT_BLOCK, ATT_BLOCK)), const2((ATT_BLOCK, SG_WIDTH)),
        const2((ATTN_WIDTH, D_MODEL)), const2((SG_WIDTH, D_MODEL)),
        const2((D_MODEL, D_MODEL)), const2((1, D_MODEL)),
        const2((1, D_MODEL)),
        const2((D_MODEL, LANES)), const2((1, LANES)),
    ]
    out_specs = [
        pl.BlockSpec((1, tm, D_MODEL), lambda b, j, s: (b, j, 0)),
        pl.BlockSpec((tm * ROW_TILES, LANES), tile_rows),
        pl.BlockSpec((tm, LANES), tile_rows),
        pl.BlockSpec((tm, LANES), tile_rows),
        pl.BlockSpec((SUBLANES, LANES), lambda b, j, s: (0, 0)),
    ]
    out_shape = [
        jax.ShapeDtypeStruct((BATCH, SEQ, D_MODEL), jnp.float32),
        jax.ShapeDtypeStruct((N_TOKENS * ROW_TILES, LANES), jnp.float32),
        jax.ShapeDtypeStruct((N_TOKENS, LANES), jnp.int32),
        jax.ShapeDtypeStruct((N_TOKENS, LANES), jnp.float32),
        jax.ShapeDtypeStruct((SUBLANES, LANES), jnp.float32),
    ]
    return pl.pallas_call(
        _mixer_kernel,
        grid_spec=pltpu.PrefetchScalarGridSpec(
            num_scalar_prefetch=1,
            grid=(BATCH, n_j),
            in_specs=in_specs,
            out_specs=out_specs,
            scratch_shapes=[
                pltpu.VMEM((ATT_BLOCK, KV_WIDTH), jnp.float32),
                pltpu.VMEM((ATT_BLOCK, KV_WIDTH), jnp.float32),
                pltpu.VMEM((SUBLANES, LANES), jnp.float32),
            ]),
        out_shape=out_shape,
        compiler_params=pltpu.CompilerParams(
            dimension_semantics=("arbitrary", "arbitrary"), vmem_limit_bytes=VMEM_LIMIT_BYTES),
        name="mixer_router",
    )(p["sinks"], x, mod.reshape(BATCH, 1, 6 * D_MODEL), p["g_pre_mix"], p["g_post_mix"],
      p["w_in"], p["b_in"], _attention_bias(), p["sg_ln_g"], p["sg_ln_b"], p["sg_w"], p["sg_b"],
      p["w_br_attn"], p["w_br_sg"], p["w_out"], p["b_out"], p["g_pre_ffn"],
      p["w_router"], p["b_router"])


def _row_copy(src, src_row, dst, dst_row, sem):
    return pltpu.make_async_copy(
        src.at[pl.ds(pl.multiple_of(src_row * ROW_TILES, ROW_TILES), ROW_TILES)],
        dst.at[pl.ds(pl.multiple_of(dst_row * ROW_TILES, ROW_TILES), ROW_TILES)], sem)


def _wait_rows(src, dst, n_rows, sem):
    pltpu.make_async_copy(src.at[pl.ds(0, n_rows * ROW_TILES)],
                          dst.at[pl.ds(0, n_rows * ROW_TILES)], sem).wait()


def _rows_to_matrix(rows_ref, first_row, n_rows):
    return jnp.concatenate(
        [rows_ref[pl.ds(first_row * ROW_TILES + s, n_rows, stride=ROW_TILES), :]
         for s in range(ROW_TILES)], axis=1)


def _dispatch_kernel(slot_ref, pad_ref, h2_hbm, xs_hbm, sems):
    i = pl.program_id(0)
    n_steps = pl.num_programs(0)
    rows_per_step = DSP_TILE * TOP_K + DSP_PADS
    sem = sems.at[i % 2]
    tok0 = i * DSP_TILE

    def issue(g, carry):
        for r in range(DMA_UNROLL):
            a = g * DMA_UNROLL + r
            tok = tok0 + g * (DMA_UNROLL // TOP_K) + r // TOP_K
            _row_copy(h2_hbm, tok, xs_hbm, slot_ref[0, 0, a], sem).start(priority=r % 2)
        return carry

    lax.fori_loop(0, DSP_TILE * TOP_K // DMA_UNROLL, issue, 0)

    def issue_pad(g, carry):
        for r in range(DMA_UNROLL):
            _row_copy(h2_hbm, 0, xs_hbm, pad_ref[0, 0, g * DMA_UNROLL + r], sem).start(priority=r % 2)
        return carry

    lax.fori_loop(0, DSP_PADS // DMA_UNROLL, issue_pad, 0)

    @pl.when(i > 0)
    def _():
        _wait_rows(h2_hbm, xs_hbm, rows_per_step, sems.at[(i + 1) % 2])

    @pl.when(i == n_steps - 1)
    def _():
        _wait_rows(h2_hbm, xs_hbm, rows_per_step, sem)


def _dispatch(slot, pad_slots, h2_rows):
    n_steps = N_TOKENS // DSP_TILE
    return pl.pallas_call(
        _dispatch_kernel,
        grid=(n_steps,),
        in_specs=[
            pl.BlockSpec((1, 1, DSP_TILE * TOP_K), lambda i: (i, 0, 0), memory_space=pltpu.SMEM),
            pl.BlockSpec((1, 1, DSP_PADS), lambda i: (i, 0, 0), memory_space=pltpu.SMEM),
            pl.BlockSpec(memory_space=pl.ANY),
        ],
        out_specs=pl.BlockSpec(memory_space=pl.ANY),
        out_shape=jax.ShapeDtypeStruct((N_SLOTS * ROW_TILES, LANES), jnp.float32),
        scratch_shapes=[pltpu.SemaphoreType.DMA((2,))],
        compiler_params=pltpu.CompilerParams(dimension_semantics=("arbitrary",)),
        name="dispatch",
    )(slot.reshape(n_steps, 1, DSP_TILE * TOP_K), pad_slots.reshape(n_steps, 1, DSP_PADS), h2_rows)


def _expert_kernel(be_ref, xs_ref, w1_ref, b1_ref, w2_ref, b2_ref, ys_ref):
    del be_ref
    xb = _bf16(_rows_to_matrix(xs_ref, 0, MOE_BLOCK))
    hb = _dot(xb, w1_ref[0]) + b1_ref[0]
    g = jnp.minimum(hb[:, :D_EXPERT], SWIGLU_LIMIT)
    lin = jnp.clip(hb[:, D_EXPERT:], -SWIGLU_LIMIT, SWIGLU_LIMIT)
    act = g * jax.nn.sigmoid(SWIGLU_ALPHA * g) * (lin + 1.0)
    y = _dot(_bf16(act), w2_ref[0]) + b2_ref[0]
    for s in range(ROW_TILES):
        ys_ref[pl.ds(s, MOE_BLOCK, stride=ROW_TILES), :] = y[:, s * LANES:(s + 1) * LANES]


def _experts(block_e, xs_rows, w1, b1, w2, b2):
    return pl.pallas_call(
        _expert_kernel,
        grid_spec=pltpu.PrefetchScalarGridSpec(
            num_scalar_prefetch=1,
            grid=(N_MOE_BLOCKS,),
            in_specs=[
                pl.BlockSpec((MOE_BLOCK * ROW_TILES, LANES), lambda i, be: (i, 0)),
                pl.BlockSpec((1, D_MODEL, 2 * D_EXPERT), lambda i, be: (be[i], 0, 0)),
                pl.BlockSpec((1, 1, 2 * D_EXPERT), lambda i, be: (be[i], 0, 0)),
                pl.BlockSpec((1, D_EXPERT, D_MODEL), lambda i, be: (be[i], 0, 0)),
                pl.BlockSpec((1, 1, D_MODEL), lambda i, be: (be[i], 0, 0)),
            ],
            out_specs=pl.BlockSpec((MOE_BLOCK * ROW_TILES, LANES), lambda i, be: (i, 0)),
        ),
        out_shape=jax.ShapeDtypeStruct((N_SLOTS * ROW_TILES, LANES), jnp.float32),
        compiler_params=pltpu.CompilerParams(
            dimension_semantics=("arbitrary",), vmem_limit_bytes=VMEM_LIMIT_BYTES),
        name="experts",
    )(block_e, xs_rows,
      w1, b1.reshape(N_EXPERTS, 1, 2 * D_EXPERT), w2, b2.reshape(N_EXPERTS, 1, D_MODEL))


def _combine_kernel(slot0_ref, slot_next_ref, ys_hbm, x1_ref, rg_ref, mod_ref, gpost_ref, o_ref,
                    buf_ref, sems):
    tm = CMB_TILE
    n_rows = TOP_K * tm
    i = pl.program_id(0)
    n_steps = pl.num_programs(0)

    def gather(slot_ref, half):
        def issue(g, carry):
            for r in range(DMA_UNROLL):
                a = g * DMA_UNROLL + r
                _row_copy(ys_hbm, slot_ref[0, 0, a], buf_ref.at[half], a, sems.at[half]).start(
                    priority=r % 2)
            return carry
        lax.fori_loop(0, n_rows // DMA_UNROLL, issue, 0)

    @pl.when(i == 0)
    def _():
        gather(slot0_ref, 0)

    @pl.when(i + 1 < n_steps)
    def _():
        gather(slot_next_ref, (i + 1) % 2)

    cur = i % 2
    _wait_rows(ys_hbm, buf_ref.at[cur], n_rows, sems.at[cur])
    gates = rg_ref[...]
    rows = buf_ref.at[cur]
    y = jnp.zeros((tm, D_MODEL), jnp.float32)
    for kk in range(TOP_K):
        y = y + _rows_to_matrix(rows, kk * tm, tm) * gates[:, kk:kk + 1]
    gate2 = mod_ref[0][:, 5 * D_MODEL:6 * D_MODEL]
    o_ref[...] = x1_ref[...] + gate2 * _rms(y, gpost_ref[...])


def _combine(slot_kmajor, ys_rows, x1, route_g, mod, g_post_ffn):
    tm = CMB_TILE
    n_tiles = N_TOKENS // tm
    tiles_per_seq = SEQ // tm
    return pl.pallas_call(
        _combine_kernel,
        grid=(n_tiles,),
        in_specs=[
            pl.BlockSpec((1, 1, TOP_K * tm), lambda i: (0, 0, 0), memory_space=pltpu.SMEM),
            pl.BlockSpec((1, 1, TOP_K * tm), lambda i: (jnp.minimum(i + 1, n_tiles - 1), 0, 0),
                         memory_space=pltpu.SMEM),
            pl.BlockSpec(memory_space=pl.ANY),
            pl.BlockSpec((tm, D_MODEL), lambda i: (i, 0)),
            pl.BlockSpec((tm, LANES), lambda i: (i, 0)),
            pl.BlockSpec((1, 1, 6 * D_MODEL), lambda i: (i // tiles_per_seq, 0, 0)),
            pl.BlockSpec((1, D_MODEL), lambda i: (0, 0)),
        ],
        out_specs=pl.BlockSpec((tm, D_MODEL), lambda i: (i, 0)),
        out_shape=jax.ShapeDtypeStruct((N_TOKENS, D_MODEL), jnp.float32),
        scratch_shapes=[
            pltpu.VMEM((2, TOP_K * tm * ROW_TILES, LANES), jnp.float32),
            pltpu.SemaphoreType.DMA((2,)),
        ],
        compiler_params=pltpu.CompilerParams(
            dimension_semantics=("arbitrary",), vmem_limit_bytes=VMEM_LIMIT_BYTES),
        name="combine",
    )(slot_kmajor, slot_kmajor, ys_rows, x1, route_g, mod.reshape(BATCH, 1, 6 * D_MODEL),
      g_post_ffn)


def kernel(x, c, w_mod, b_mod, g_pre_mix, g_post_mix, w_in, b_in, attn_sinks, sg_ln_g, sg_ln_b,
           sg_w, sg_b, w_br_attn, w_br_sg, w_out, b_out, g_pre_ffn, g_post_ffn, w_router,
           b_router, w_mlp1, b_mlp1, w_mlp2, b_mlp2):
    assert x.shape == (BATCH, SEQ, D_MODEL) and w_mod.shape[0] == 1, "single-layer shapes only"
    row = lambda a: a[0].reshape(1, -1)
    bf = lambda a: a[0].astype(jnp.bfloat16)

    mod = _modulation(c, w_mod[0], b_mod[0])

    params = dict(
        sinks=attn_sinks[0], g_pre_mix=row(g_pre_mix), g_post_mix=row(g_post_mix),
        w_in=bf(w_in), b_in=row(b_in), sg_ln_g=row(sg_ln_g), sg_ln_b=row(sg_ln_b), sg_w=sg_w[0],
        sg_b=jnp.repeat(sg_b[0].T, SG_GROUP_DIM, axis=1),
        w_br_attn=bf(w_br_attn), w_br_sg=bf(w_br_sg), w_out=bf(w_out), b_out=row(b_out),
        g_pre_ffn=row(g_pre_ffn),
        w_router=jnp.pad(bf(w_router), ((0, 0), (0, LANES - N_EXPERTS))),
        b_router=jnp.pad(row(b_router), ((0, 0), (0, LANES - N_EXPERTS)), constant_values=_NEG_INF),
    )
    x1, h2_rows, route_i, route_g, counts = _mixer(x, mod, params)

    counts = counts[0, :N_EXPERTS].astype(jnp.int32)
    padded = (counts + MOE_BLOCK - 1) // MOE_BLOCK * MOE_BLOCK
    pends = jnp.cumsum(padded)
    pstarts = pends - padded
    block_start = jnp.arange(N_MOE_BLOCKS, dtype=jnp.int32) * MOE_BLOCK
    block_e = jnp.minimum(jnp.sum((pends[None, :] <= block_start[:, None]).astype(jnp.int32), axis=1),
                          N_EXPERTS - 1)
    expert_ids = jnp.arange(N_EXPERTS, dtype=jnp.int32)
    top_idx = route_i[:, :TOP_K]
    slot = (jnp.sum(jnp.where(top_idx[:, :, None] == expert_ids, pstarts, 0), axis=-1)
            + route_i[:, TOP_K:2 * TOP_K])
    free_start = jnp.concatenate([pstarts + counts, pends[-1:]])
    free_len = jnp.concatenate([padded - counts, N_SLOTS - pends[-1:]])
    free_end = jnp.cumsum(free_len)
    q = jnp.arange(N_PAD_SLOTS, dtype=jnp.int32)
    region = jnp.sum((free_end[None, :] <= q[:, None]).astype(jnp.int32), axis=1)
    region_hit = region[:, None] == jnp.arange(N_EXPERTS + 1, dtype=jnp.int32)
    pad_slots = q + jnp.sum(jnp.where(region_hit, free_start - (free_end - free_len), 0), axis=1)

    xs_rows = _dispatch(slot, pad_slots, h2_rows)
    ys_rows = _experts(block_e, xs_rows, bf(w_mlp1), b_mlp1[0], bf(w_mlp2), b_mlp2[0])

    n_tiles = N_TOKENS // CMB_TILE
    slot_kmajor = slot.reshape(n_tiles, CMB_TILE, TOP_K).transpose(0, 2, 1).reshape(
        n_tiles, 1, TOP_K * CMB_TILE)
    out = _combine(slot_kmajor, ys_rows, x1.reshape(N_TOKENS, D_MODEL), route_g, mod, row(g_post_ffn))
    return out.reshape(BATCH, SEQ, D_MODEL)
```

```python
import functools

import jax
import jax.numpy as jnp
import numpy as np
from jax import lax
from jax.experimental import pallas as pl
from jax.experimental.pallas import tpu as pltpu

D_MODEL = 1024
BATCH = 16
SEQ = 2048
N_TOKENS = BATCH * SEQ
HEAD_DIM = 64
N_Q_HEADS = 8
N_KV_HEADS = 2
Q_PER_KV = N_Q_HEADS // N_KV_HEADS
ATTN_WIDTH = N_Q_HEADS * HEAD_DIM
KV_WIDTH = N_KV_HEADS * HEAD_DIM
ATT_BLOCK = 128
N_SG_GROUPS = 8
SG_GROUP_DIM = 64
SG_WIDTH = N_SG_GROUPS * SG_GROUP_DIM
N_EXPERTS = 32
TOP_K = 4
D_EXPERT = D_MODEL
SWIGLU_LIMIT = 7.0
SWIGLU_ALPHA = 1.702
MOE_BLOCK = 256
NORM_EPS = 1e-6
N_ASSIGN = N_TOKENS * TOP_K
N_MOE_BLOCKS = N_ASSIGN // MOE_BLOCK + N_EXPERTS
N_SLOTS = N_MOE_BLOCKS * MOE_BLOCK

_Q0 = 0
_K0 = _Q0 + ATTN_WIDTH
_V0 = _K0 + KV_WIDTH
_U0 = _V0 + KV_WIDTH
_SV0 = _U0 + SG_WIDTH
_GA0 = _SV0 + SG_WIDTH
_GS0 = _GA0 + D_MODEL
IN_WIDTH = _GS0 + D_MODEL

LANES = 128
SUBLANES = 8
ROW_TILES = D_MODEL // LANES
assert ROW_TILES == SUBLANES
VMEM_LIMIT_BYTES = 56 * 1024 * 1024

MIX_TILE = 256
CMB_TILE = 128
DSP_TILE = 256
DMA_UNROLL = 8
N_PAD_SLOTS = N_SLOTS - N_ASSIGN
DSP_PADS = N_PAD_SLOTS // (N_TOKENS // DSP_TILE)
assert DMA_UNROLL % TOP_K == 0 and DSP_PADS % DMA_UNROLL == 0

_NEG_INF = float("-inf")


def _rms(x, g):
    return x * lax.rsqrt(jnp.mean(x * x, axis=-1, keepdims=True) + NORM_EPS) * g


def _bf16(x):
    return x.astype(jnp.bfloat16)


def _dot(a, b):
    return jnp.dot(a, b, preferred_element_type=jnp.float32)


def _mod_kernel(c_ref, w_ref, b_ref, o_ref):
    c = c_ref[...]
    c_act = c * jax.nn.sigmoid(c)
    o_ref[...] = _dot(_bf16(c_act), _bf16(w_ref[...])) + b_ref[...]


def _modulation(c, w_mod, b_mod):
    n_out = w_mod.shape[1]
    tn = D_MODEL
    return pl.pallas_call(
        _mod_kernel,
        grid=(n_out // tn,),
        in_specs=[
            pl.BlockSpec((BATCH, D_MODEL), lambda n: (0, 0)),
            pl.BlockSpec((D_MODEL, tn), lambda n: (0, n)),
            pl.BlockSpec((1, tn), lambda n: (0, n)),
        ],
        out_specs=pl.BlockSpec((BATCH, tn), lambda n: (0, n)),
        out_shape=jax.ShapeDtypeStruct((BATCH, n_out), jnp.float32),
        compiler_params=pltpu.CompilerParams(dimension_semantics=("arbitrary",)),
        name="modulation",
    )(c, w_mod, b_mod.reshape(1, n_out))


def _attention_bias():
    slopes = 2.0 ** (-8.0 * np.arange(1, N_Q_HEADS + 1) / N_Q_HEADS)
    qi = np.arange(ATT_BLOCK)[:, None]
    kj = np.arange(2 * ATT_BLOCK)[None, :]
    dist = qi + ATT_BLOCK - kj
    in_win = (dist >= 0) & (dist < ATT_BLOCK)
    bias = np.where(in_win[None], -slopes[:, None, None] * dist[None].astype(np.float64), -np.inf)
    bias = bias.reshape(N_KV_HEADS, Q_PER_KV * ATT_BLOCK, 2 * ATT_BLOCK)
    return jnp.asarray(bias, dtype=jnp.float32)


def _mixer_kernel(sinks_ref, x_ref, mod_ref, gpre_ref, gpost_ref, win_ref, bin_ref, abias_ref,
                  lng_ref, lnb_ref, sgw_ref, sgb_ref, wba_ref, wbs_ref, wout_ref, bout_ref,
                  gffn_ref, wr_ref, br_ref,
                  x1_ref, h2_ref, ri_ref, rg_ref, cnt_ref,
                  kprev_ref, vprev_ref, carry_ref):
    b = pl.program_id(0)
    j = pl.program_id(1)
    tm = MIX_TILE
    n_blk = tm // ATT_BLOCK

    @pl.when((b == 0) & (j == 0))
    def _():
        carry_ref[...] = jnp.zeros_like(carry_ref)

    @pl.when(j == 0)
    def _():
        kprev_ref[...] = jnp.zeros_like(kprev_ref)
        vprev_ref[...] = jnp.zeros_like(vprev_ref)

    x = x_ref[0]
    mod = mod_ref[0]
    shift1, scale1, gate1, shift2, scale2, gate2 = (
        mod[:, i * D_MODEL:(i + 1) * D_MODEL] for i in range(6))

    h = _bf16(_rms(x, gpre_ref[...]) * (1.0 + scale1) + shift1)

    def proj(lo, width):
        return _dot(h, win_ref[:, lo:lo + width]) + bin_ref[:, lo:lo + width]

    q = proj(_Q0, ATTN_WIDTH)
    k = proj(_K0, KV_WIDTH)
    v = proj(_V0, KV_WIDTH)
    row_group = lax.broadcasted_iota(jnp.int32, (Q_PER_KV * ATT_BLOCK, 1), 0) // ATT_BLOCK
    col = lax.broadcasted_iota(jnp.int32, (Q_PER_KV * ATT_BLOCK, 2 * ATT_BLOCK), 1)
    y_att_blocks = []
    for i in range(n_blk):
        rows = slice(i * ATT_BLOCK, (i + 1) * ATT_BLOCK)
        if i == 0:
            k_prev, v_prev = kprev_ref[...], vprev_ref[...]
        else:
            prev = slice((i - 1) * ATT_BLOCK, i * ATT_BLOCK)
            k_prev, v_prev = k[prev], v[prev]
        k_cat = _bf16(jnp.concatenate([k_prev, k[rows]], axis=0))
        v_cat = _bf16(jnp.concatenate([v_prev, v[rows]], axis=0))
        q_blk = _bf16(q[rows])
        heads = [None] * N_Q_HEADS
        for hk in range(N_KV_HEADS):
            kv_cols = slice(hk * HEAD_DIM, (hk + 1) * HEAD_DIM)
            q_stack = jnp.concatenate(
                [q_blk[:, (hk * Q_PER_KV + g) * HEAD_DIM:(hk * Q_PER_KV + g + 1) * HEAD_DIM]
                 for g in range(Q_PER_KV)], axis=0)
            scores = lax.dot_general(q_stack, k_cat[:, kv_cols], (((1,), (1,)), ((), ())),
                                     preferred_element_type=jnp.float32)
            logits = scores * (HEAD_DIM ** -0.5) + abias_ref[hk]
            if i == 0:
                n_masked = jnp.where(j == 0, ATT_BLOCK, 0)
                logits = jnp.where(col < n_masked, _NEG_INF, logits)
            sink = jnp.zeros((Q_PER_KV * ATT_BLOCK, 1), jnp.float32)
            for g in range(Q_PER_KV):
                sink = jnp.where(row_group == g, sinks_ref[hk * Q_PER_KV + g], sink)
            m = jnp.maximum(jnp.max(logits, axis=-1, keepdims=True), sink)
            p = jnp.exp(logits - m)
            probs = p / (jnp.sum(p, axis=-1, keepdims=True) + jnp.exp(sink - m))
            out = _dot(_bf16(probs), v_cat[:, kv_cols])
            for g in range(Q_PER_KV):
                heads[hk * Q_PER_KV + g] = out[g * ATT_BLOCK:(g + 1) * ATT_BLOCK]
        y_att_blocks.append(jnp.concatenate(heads, axis=1))
    kprev_ref[...] = k[tm - ATT_BLOCK:]
    vprev_ref[...] = v[tm - ATT_BLOCK:]
    y_att = jnp.concatenate(y_att_blocks, axis=0)
    branch_att = _dot(_bf16(y_att), wba_ref[...])

    u = jax.nn.gelu(proj(_U0, SG_WIDTH))
    sv = jax.nn.gelu(proj(_SV0, SG_WIDTH))
    mu = jnp.mean(sv, axis=-1, keepdims=True)
    var = jnp.mean(jnp.square(sv - mu), axis=-1, keepdims=True)
    vn = _bf16((sv - mu) * lax.rsqrt(var + NORM_EPS) * lng_ref[...] + lnb_ref[...])
    causal = (lax.broadcasted_iota(jnp.int32, (ATT_BLOCK, ATT_BLOCK), 0)
              >= lax.broadcasted_iota(jnp.int32, (ATT_BLOCK, ATT_BLOCK), 1))
    w_sp = [_bf16(jnp.where(causal, sgw_ref[g], 0.0)) for g in range(N_SG_GROUPS)]
    mixed_chunks = []
    for cidx in range(n_blk):
        rows = slice(cidx * ATT_BLOCK, (cidx + 1) * ATT_BLOCK)
        groups = [_dot(w_sp[g], vn[rows, g * SG_GROUP_DIM:(g + 1) * SG_GROUP_DIM])
                  for g in range(N_SG_GROUPS)]
        mixed_chunks.append(jnp.concatenate(groups, axis=1) + sgb_ref[...])
    y_sg = u * jnp.concatenate(mixed_chunks, axis=0)
    branch_sg = _dot(_bf16(y_sg), wbs_ref[...])

    merged = (jax.nn.sigmoid(proj(_GA0, D_MODEL)) * branch_att
              + jax.nn.sigmoid(proj(_GS0, D_MODEL)) * branch_sg)
    y = _dot(_bf16(merged), wout_ref[...]) + bout_ref[...]
    x1 = x + gate1 * _rms(y, gpost_ref[...])
    x1_ref[0] = x1

    h2 = _rms(x1, gffn_ref[...]) * (1.0 + scale2) + shift2
    for s in range(ROW_TILES):
        h2_ref[pl.ds(s, tm, stride=ROW_TILES), :] = h2[:, s * LANES:(s + 1) * LANES]
    logits = _dot(_bf16(h2), wr_ref[...]) + br_ref[...]
    lane = lax.broadcasted_iota(jnp.int32, (tm, LANES), 1).astype(jnp.float32)
    top_val, top_idx = [], []
    selected = jnp.zeros((tm, LANES), jnp.float32)
    for _ in range(TOP_K):
        mval = jnp.max(logits, axis=-1, keepdims=True)
        midx = jnp.min(jnp.where(logits == mval, lane, float(LANES)), axis=-1, keepdims=True)
        hit = lane == midx
        selected = jnp.where(hit, 1.0, selected)
        logits = jnp.where(hit, _NEG_INF, logits)
        top_val.append(mval)
        top_idx.append(midx)
    expv = [jnp.exp(tv - top_val[0]) for tv in top_val]
    denom = expv[0] + expv[1] + expv[2] + expv[3]
    strict_lower = (lax.broadcasted_iota(jnp.int32, (tm, tm), 0)
                    > lax.broadcasted_iota(jnp.int32, (tm, tm), 1))
    before = _dot(_bf16(jnp.where(strict_lower, 1.0, 0.0)), _bf16(selected)) + carry_ref[0:1, :]
    route_i = jnp.zeros((tm, LANES), jnp.float32)
    route_g = jnp.zeros((tm, LANES), jnp.float32)
    for kk in range(TOP_K):
        rank = jnp.sum(jnp.where(lane == top_idx[kk], before, 0.0), axis=-1, keepdims=True)
        route_i = jnp.where(lane == float(kk), top_idx[kk], route_i)
        route_i = jnp.where(lane == float(TOP_K + kk), rank, route_i)
        route_g = jnp.where(lane == float(kk), expv[kk] / denom, route_g)
    ri_ref[...] = route_i.astype(jnp.int32)
    rg_ref[...] = route_g
    total = carry_ref[0:1, :] + jnp.sum(selected, axis=0, keepdims=True)
    carry_ref[...] = jnp.broadcast_to(total, carry_ref.shape)
    cnt_ref[...] = jnp.broadcast_to(total, cnt_ref.shape)


def _mixer(x, mod, p):
    tm = MIX_TILE
    n_j = SEQ // tm
    const2 = lambda shape: pl.BlockSpec(shape, lambda b, j, s: (0, 0), pipeline_mode=pl.Buffered(1))
    const3 = lambda shape: pl.BlockSpec(shape, lambda b, j, s: (0, 0, 0), pipeline_mode=pl.Buffered(1))
    tile_rows = lambda b, j, s: (b * n_j + j, 0)
    in_specs = [
        pl.BlockSpec((1, tm, D_MODEL), lambda b, j, s: (b, j, 0)),
        pl.BlockSpec((1, 1, 6 * D_MODEL), lambda b, j, s: (b, 0, 0)),
        const2((1, D_MODEL)), const2((1, D_MODEL)),
        const2((D_MODEL, IN_WIDTH)), const2((1, IN_WIDTH)),
        const3((N_KV_HEADS, Q_PER_KV * ATT_BLOCK, 2 * ATT_BLOCK)),
        const2((1, SG_WIDTH)), const2((1, SG_WIDTH)),
        const3((N_SG_GROUPS, ATT_BLOCK, ATT_BLOCK)), const2((ATT_BLOCK, SG_WIDTH)),
        const2((ATTN_WIDTH, D_MODEL)), const2((SG_WIDTH, D_MODEL)),
        const2((D_MODEL, D_MODEL)), const2((1, D_MODEL)),
        const2((1, D_MODEL)),
        const2((D_MODEL, LANES)), const2((1, LANES)),
    ]
    out_specs = [
        pl.BlockSpec((1, tm, D_MODEL), lambda b, j, s: (b, j, 0)),
        pl.BlockSpec((tm * ROW_TILES, LANES), tile_rows),
        pl.BlockSpec((tm, LANES), tile_rows),
        pl.BlockSpec((tm, LANES), tile_rows),
        pl.BlockSpec((SUBLANES, LANES), lambda b, j, s: (0, 0)),
    ]
    out_shape = [
        jax.ShapeDtypeStruct((BATCH, SEQ, D_MODEL), jnp.float32),
        jax.ShapeDtypeStruct((N_TOKENS * ROW_TILES, LANES), jnp.float32),
        jax.ShapeDtypeStruct((N_TOKENS, LANES), jnp.int32),
        jax.ShapeDtypeStruct((N_TOKENS, LANES), jnp.float32),
        jax.ShapeDtypeStruct((SUBLANES, LANES), jnp.float32),
    ]
    return pl.pallas_call(
        _mixer_kernel,
        grid_spec=pltpu.PrefetchScalarGridSpec(
            num_scalar_prefetch=1,
            grid=(BATCH, n_j),
            in_specs=in_specs,
            out_specs=out_specs,
            scratch_shapes=[
                pltpu.VMEM((ATT_BLOCK, KV_WIDTH), jnp.float32),
                pltpu.VMEM((ATT_BLOCK, KV_WIDTH), jnp.float32),
                pltpu.VMEM((SUBLANES, LANES), jnp.float32),
            ]),
        out_shape=out_shape,
        compiler_params=pltpu.CompilerParams(
            dimension_semantics=("arbitrary", "arbitrary"), vmem_limit_bytes=VMEM_LIMIT_BYTES),
        name="mixer_router",
    )(p["sinks"], x, mod.reshape(BATCH, 1, 6 * D_MODEL), p["g_pre_mix"], p["g_post_mix"],
      p["w_in"], p["b_in"], _attention_bias(), p["sg_ln_g"], p["sg_ln_b"], p["sg_w"], p["sg_b"],
      p["w_br_attn"], p["w_br_sg"], p["w_out"], p["b_out"], p["g_pre_ffn"],
      p["w_router"], p["b_router"])


def _row_copy(src, src_row, dst, dst_row, sem):
    return pltpu.make_async_copy(
        src.at[pl.ds(pl.multiple_of(src_row * ROW_TILES, ROW_TILES), ROW_TILES)],
        dst.at[pl.ds(pl.multiple_of(dst_row * ROW_TILES, ROW_TILES), ROW_TILES)], sem)


def _wait_rows(src, dst, n_rows, sem):
    pltpu.make_async_copy(src.at[pl.ds(0, n_rows * ROW_TILES)],
                          dst.at[pl.ds(0, n_rows * ROW_TILES)], sem).wait()


def _rows_to_matrix(rows_ref, first_row, n_rows):
    return jnp.concatenate(
        [rows_ref[pl.ds(first_row * ROW_TILES + s, n_rows, stride=ROW_TILES), :]
         for s in range(ROW_TILES)], axis=1)


def _dispatch_kernel(slot_ref, pad_ref, h2_ref, xs_hbm, sem):
    def issue(g, carry):
        for r in range(DMA_UNROLL):
            a = g * DMA_UNROLL + r
            tok = g * (DMA_UNROLL // TOP_K) + r // TOP_K
            _row_copy(h2_ref, tok, xs_hbm, slot_ref[0, 0, a], sem).start(priority=r % 2)
        return carry

    lax.fori_loop(0, DSP_TILE * TOP_K // DMA_UNROLL, issue, 0)

    def issue_pad(g, carry):
        for r in range(DMA_UNROLL):
            _row_copy(h2_ref, 0, xs_hbm, pad_ref[0, 0, g * DMA_UNROLL + r], sem).start(priority=r % 2)
        return carry

    lax.fori_loop(0, DSP_PADS // DMA_UNROLL, issue_pad, 0)
    n_rows = (DSP_TILE * TOP_K + DSP_PADS) * ROW_TILES
    pltpu.make_async_copy(xs_hbm.at[pl.ds(0, n_rows)], xs_hbm.at[pl.ds(n_rows, n_rows)], sem).wait()


def _dispatch(slot, pad_slots, h2_rows):
    n_steps = N_TOKENS // DSP_TILE
    return pl.pallas_call(
        _dispatch_kernel,
        grid=(n_steps,),
        in_specs=[
            pl.BlockSpec((1, 1, DSP_TILE * TOP_K), lambda i: (i, 0, 0), memory_space=pltpu.SMEM),
            pl.BlockSpec((1, 1, DSP_PADS), lambda i: (i, 0, 0), memory_space=pltpu.SMEM),
            pl.BlockSpec((DSP_TILE * ROW_TILES, LANES), lambda i: (i, 0)),
        ],
        out_specs=pl.BlockSpec(memory_space=pl.ANY),
        out_shape=jax.ShapeDtypeStruct((N_SLOTS * ROW_TILES, LANES), jnp.float32),
        scratch_shapes=[pltpu.SemaphoreType.DMA],
        compiler_params=pltpu.CompilerParams(dimension_semantics=("arbitrary",)),
        name="dispatch",
    )(slot.reshape(n_steps, 1, DSP_TILE * TOP_K), pad_slots.reshape(n_steps, 1, DSP_PADS), h2_rows)


def _expert_kernel(be_ref, xs_ref, w1_ref, b1_ref, w2_ref, b2_ref, ys_ref):
    del be_ref
    xb = _bf16(_rows_to_matrix(xs_ref, 0, MOE_BLOCK))
    hb = _dot(xb, w1_ref[0]) + b1_ref[0]
    g = jnp.minimum(hb[:, :D_EXPERT], SWIGLU_LIMIT)
    lin = jnp.clip(hb[:, D_EXPERT:], -SWIGLU_LIMIT, SWIGLU_LIMIT)
    act = g * jax.nn.sigmoid(SWIGLU_ALPHA * g) * (lin + 1.0)
    y = _dot(_bf16(act), w2_ref[0]) + b2_ref[0]
    for s in range(ROW_TILES):
        ys_ref[pl.ds(s, MOE_BLOCK, stride=ROW_TILES), :] = y[:, s * LANES:(s + 1) * LANES]


def _experts(block_e, xs_rows, w1, b1, w2, b2):
    return pl.pallas_call(
        _expert_kernel,
        grid_spec=pltpu.PrefetchScalarGridSpec(
            num_scalar_prefetch=1,
            grid=(N_MOE_BLOCKS,),
            in_specs=[
                pl.BlockSpec((MOE_BLOCK * ROW_TILES, LANES), lambda i, be: (i, 0)),
                pl.BlockSpec((1, D_MODEL, 2 * D_EXPERT), lambda i, be: (be[i], 0, 0)),
                pl.BlockSpec((1, 1, 2 * D_EXPERT), lambda i, be: (be[i], 0, 0)),
                pl.BlockSpec((1, D_EXPERT, D_MODEL), lambda i, be: (be[i], 0, 0)),
                pl.BlockSpec((1, 1, D_MODEL), lambda i, be: (be[i], 0, 0)),
            ],
            out_specs=pl.BlockSpec((MOE_BLOCK * ROW_TILES, LANES), lambda i, be: (i, 0)),
        ),
        out_shape=jax.ShapeDtypeStruct((N_SLOTS * ROW_TILES, LANES), jnp.float32),
        compiler_params=pltpu.CompilerParams(
            dimension_semantics=("arbitrary",), vmem_limit_bytes=VMEM_LIMIT_BYTES),
        name="experts",
    )(block_e, xs_rows,
      w1, b1.reshape(N_EXPERTS, 1, 2 * D_EXPERT), w2, b2.reshape(N_EXPERTS, 1, D_MODEL))


def _combine_kernel(slot0_ref, slot_next_ref, ys_hbm, x1_ref, rg_ref, mod_ref, gpost_ref, o_ref,
                    buf_ref, sems):
    tm = CMB_TILE
    n_rows = TOP_K * tm
    i = pl.program_id(0)
    n_steps = pl.num_programs(0)

    def gather(slot_ref, half):
        def issue(g, carry):
            for r in range(DMA_UNROLL):
                a = g * DMA_UNROLL + r
                _row_copy(ys_hbm, slot_ref[0, 0, a], buf_ref.at[half], a, sems.at[half]).start(
                    priority=r % 2)
            return carry
        lax.fori_loop(0, n_rows // DMA_UNROLL, issue, 0)

    @pl.when(i == 0)
    def _():
        gather(slot0_ref, 0)

    @pl.when(i + 1 < n_steps)
    def _():
        gather(slot_next_ref, (i + 1) % 2)

    cur = i % 2
    _wait_rows(ys_hbm, buf_ref.at[cur], n_rows, sems.at[cur])
    gates = rg_ref[...]
    rows = buf_ref.at[cur]
    y = jnp.zeros((tm, D_MODEL), jnp.float32)
    for kk in range(TOP_K):
        y = y + _rows_to_matrix(rows, kk * tm, tm) * gates[:, kk:kk + 1]
    gate2 = mod_ref[0][:, 5 * D_MODEL:6 * D_MODEL]
    o_ref[...] = x1_ref[...] + gate2 * _rms(y, gpost_ref[...])


def _combine(slot_kmajor, ys_rows, x1, route_g, mod, g_post_ffn):
    tm = CMB_TILE
    n_tiles = N_TOKENS // tm
    tiles_per_seq = SEQ // tm
    return pl.pallas_call(
        _combine_kernel,
        grid=(n_tiles,),
        in_specs=[
            pl.BlockSpec((1, 1, TOP_K * tm), lambda i: (0, 0, 0), memory_space=pltpu.SMEM),
            pl.BlockSpec((1, 1, TOP_K * tm), lambda i: (jnp.minimum(i + 1, n_tiles - 1), 0, 0),
                         memory_space=pltpu.SMEM),
            pl.BlockSpec(memory_space=pl.ANY),
            pl.BlockSpec((tm, D_MODEL), lambda i: (i, 0)),
            pl.BlockSpec((tm, LANES), lambda i: (i, 0)),
            pl.BlockSpec((1, 1, 6 * D_MODEL), lambda i: (i // tiles_per_seq, 0, 0)),
            pl.BlockSpec((1, D_MODEL), lambda i: (0, 0)),
        ],
        out_specs=pl.BlockSpec((tm, D_MODEL), lambda i: (i, 0)),
        out_shape=jax.ShapeDtypeStruct((N_TOKENS, D_MODEL), jnp.float32),
        scratch_shapes=[
            pltpu.VMEM((2, TOP_K * tm * ROW_TILES, LANES), jnp.float32),
            pltpu.SemaphoreType.DMA((2,)),
        ],
        compiler_params=pltpu.CompilerParams(
            dimension_semantics=("arbitrary",), vmem_limit_bytes=VMEM_LIMIT_BYTES),
        name="combine",
    )(slot_kmajor, slot_kmajor, ys_rows, x1, route_g, mod.reshape(BATCH, 1, 6 * D_MODEL),
      g_post_ffn)


def kernel(x, c, w_mod, b_mod, g_pre_mix, g_post_mix, w_in, b_in, attn_sinks, sg_ln_g, sg_ln_b,
           sg_w, sg_b, w_br_attn, w_br_sg, w_out, b_out, g_pre_ffn, g_post_ffn, w_router,
           b_router, w_mlp1, b_mlp1, w_mlp2, b_mlp2):
    assert x.shape == (BATCH, SEQ, D_MODEL) and w_mod.shape[0] == 1, "single-layer shapes only"
    row = lambda a: a[0].reshape(1, -1)
    bf = lambda a: a[0].astype(jnp.bfloat16)

    mod = _modulation(c, w_mod[0], b_mod[0])

    params = dict(
        sinks=attn_sinks[0], g_pre_mix=row(g_pre_mix), g_post_mix=row(g_post_mix),
        w_in=bf(w_in), b_in=row(b_in), sg_ln_g=row(sg_ln_g), sg_ln_b=row(sg_ln_b), sg_w=sg_w[0],
        sg_b=jnp.repeat(sg_b[0].T, SG_GROUP_DIM, axis=1),
        w_br_attn=bf(w_br_attn), w_br_sg=bf(w_br_sg), w_out=bf(w_out), b_out=row(b_out),
        g_pre_ffn=row(g_pre_ffn),
        w_router=jnp.pad(bf(w_router), ((0, 0), (0, LANES - N_EXPERTS))),
        b_router=jnp.pad(row(b_router), ((0, 0), (0, LANES - N_EXPERTS)), constant_values=_NEG_INF),
    )
    x1, h2_rows, route_i, route_g, counts = _mixer(x, mod, params)

    counts = counts[0, :N_EXPERTS].astype(jnp.int32)
    padded = (counts + MOE_BLOCK - 1) // MOE_BLOCK * MOE_BLOCK
    pends = jnp.cumsum(padded)
    pstarts = pends - padded
    block_start = jnp.arange(N_MOE_BLOCKS, dtype=jnp.int32) * MOE_BLOCK
    block_e = jnp.minimum(jnp.sum((pends[None, :] <= block_start[:, None]).astype(jnp.int32), axis=1),
                          N_EXPERTS - 1)
    expert_ids = jnp.arange(N_EXPERTS, dtype=jnp.int32)
    top_idx = route_i[:, :TOP_K]
    slot = (jnp.sum(jnp.where(top_idx[:, :, None] == expert_ids, pstarts, 0), axis=-1)
            + route_i[:, TOP_K:2 * TOP_K])
    free_start = jnp.concatenate([pstarts + counts, pends[-1:]])
    free_len = jnp.concatenate([padded - counts, N_SLOTS - pends[-1:]])
    free_end = jnp.cumsum(free_len)
    q = jnp.arange(N_PAD_SLOTS, dtype=jnp.int32)
    region = jnp.sum((free_end[None, :] <= q[:, None]).astype(jnp.int32), axis=1)
    region_hit = region[:, None] == jnp.arange(N_EXPERTS + 1, dtype=jnp.int32)
    pad_slots = q + jnp.sum(jnp.where(region_hit, free_start - (free_end - free_len), 0), axis=1)

    xs_rows = _dispatch(slot, pad_slots, h2_rows)
    ys_rows = _experts(block_e, xs_rows, bf(w_mlp1), b_mlp1[0], bf(w_mlp2), b_mlp2[0])

    n_tiles = N_TOKENS // CMB_TILE
    slot_kmajor = slot.reshape(n_tiles, CMB_TILE, TOP_K).transpose(0, 2, 1).reshape(
        n_tiles, 1, TOP_K * CMB_TILE)
    out = _combine(slot_kmajor, ys_rows, x1.reshape(N_TOKENS, D_MODEL), route_g, mod, row(g_post_ffn))
    return out.reshape(BATCH, SEQ, D_MODEL)
```

```python
import functools

import jax
import jax.numpy as jnp
import numpy as np
from jax import lax
from jax.experimental import pallas as pl
from jax.experimental.pallas import tpu as pltpu

D_MODEL = 1024
BATCH = 16
SEQ = 2048
N_TOKENS = BATCH * SEQ
HEAD_DIM = 64
N_Q_HEADS = 8
N_KV_HEADS = 2
Q_PER_KV = N_Q_HEADS // N_KV_HEADS
ATTN_WIDTH = N_Q_HEADS * HEAD_DIM
KV_WIDTH = N_KV_HEADS * HEAD_DIM
ATT_BLOCK = 128
N_SG_GROUPS = 8
SG_GROUP_DIM = 64
SG_WIDTH = N_SG_GROUPS * SG_GROUP_DIM
N_EXPERTS = 32
TOP_K = 4
D_EXPERT = D_MODEL
SWIGLU_LIMIT = 7.0
SWIGLU_ALPHA = 1.702
MOE_BLOCK = 512
NORM_EPS = 1e-6
N_ASSIGN = N_TOKENS * TOP_K
N_MOE_BLOCKS = N_ASSIGN // MOE_BLOCK + N_EXPERTS
N_SLOTS = N_MOE_BLOCKS * MOE_BLOCK

_Q0 = 0
_K0 = _Q0 + ATTN_WIDTH
_V0 = _K0 + KV_WIDTH
_U0 = _V0 + KV_WIDTH
_SV0 = _U0 + SG_WIDTH
_GA0 = _SV0 + SG_WIDTH
_GS0 = _GA0 + D_MODEL
IN_WIDTH = _GS0 + D_MODEL

LANES = 128
SUBLANES = 8
ROW_TILES = D_MODEL // LANES
assert ROW_TILES == SUBLANES
VMEM_LIMIT_BYTES = 56 * 1024 * 1024

MIX_TILE = 512
CMB_TILE = 128
DSP_TILE = 256
DMA_UNROLL = 8
N_PAD_SLOTS = N_SLOTS - N_ASSIGN
DSP_PADS = N_PAD_SLOTS // (N_TOKENS // DSP_TILE)
assert DMA_UNROLL % TOP_K == 0 and DSP_PADS % DMA_UNROLL == 0

_NEG_INF = float("-inf")


def _rms(x, g):
    return x * lax.rsqrt(jnp.mean(x * x, axis=-1, keepdims=True) + NORM_EPS) * g


def _bf16(x):
    return x.astype(jnp.bfloat16)


def _dot(a, b):
    return jnp.dot(a, b, preferred_element_type=jnp.float32)


def _mod_kernel(c_ref, w_ref, b_ref, o_ref):
    c = c_ref[...]
    c_act = c * jax.nn.sigmoid(c)
    o_ref[...] = _dot(_bf16(c_act), _bf16(w_ref[...])) + b_ref[...]


def _modulation(c, w_mod, b_mod):
    n_out = w_mod.shape[1]
    tn = D_MODEL
    return pl.pallas_call(
        _mod_kernel,
        grid=(n_out // tn,),
        in_specs=[
            pl.BlockSpec((BATCH, D_MODEL), lambda n: (0, 0)),
            pl.BlockSpec((D_MODEL, tn), lambda n: (0, n)),
            pl.BlockSpec((1, tn), lambda n: (0, n)),
        ],
        out_specs=pl.BlockSpec((BATCH, tn), lambda n: (0, n)),
        out_shape=jax.ShapeDtypeStruct((BATCH, n_out), jnp.float32),
        compiler_params=pltpu.CompilerParams(dimension_semantics=("arbitrary",)),
        name="modulation",
    )(c, w_mod, b_mod.reshape(1, n_out))


def _attention_bias():
    slopes = 2.0 ** (-8.0 * np.arange(1, N_Q_HEADS + 1) / N_Q_HEADS)
    qi = np.arange(ATT_BLOCK)[:, None]
    kj = np.arange(2 * ATT_BLOCK)[None, :]
    dist = qi + ATT_BLOCK - kj
    in_win = (dist >= 0) & (dist < ATT_BLOCK)
    bias = np.where(in_win[None], -slopes[:, None, None] * dist[None].astype(np.float64), -np.inf)
    bias = bias.reshape(N_KV_HEADS, Q_PER_KV * ATT_BLOCK, 2 * ATT_BLOCK)
    return jnp.asarray(bias, dtype=jnp.float32)


def _mixer_kernel(sinks_ref, x_ref, mod_ref, gpre_ref, gpost_ref, win_ref, bin_ref, abias_ref,
                  lng_ref, lnb_ref, sgw_ref, sgb_ref, wba_ref, wbs_ref, wout_ref, bout_ref,
                  gffn_ref, wr_ref, br_ref,
                  x1_ref, h2_ref, ri_ref, rg_ref, cnt_ref,
                  kprev_ref, vprev_ref, carry_ref):
    b = pl.program_id(0)
    j = pl.program_id(1)
    tm = MIX_TILE
    n_blk = tm // ATT_BLOCK

    @pl.when((b == 0) & (j == 0))
    def _():
        carry_ref[...] = jnp.zeros_like(carry_ref)

    @pl.when(j == 0)
    def _():
        kprev_ref[...] = jnp.zeros_like(kprev_ref)
        vprev_ref[...] = jnp.zeros_like(vprev_ref)

    x = x_ref[0]
    mod = mod_ref[0]
    shift1, scale1, gate1, shift2, scale2, gate2 = (
        mod[:, i * D_MODEL:(i + 1) * D_MODEL] for i in range(6))

    h = _bf16(_rms(x, gpre_ref[...]) * (1.0 + scale1) + shift1)

    def proj(lo, width):
        return _dot(h, win_ref[:, lo:lo + width]) + bin_ref[:, lo:lo + width]

    q = proj(_Q0, ATTN_WIDTH)
    k = proj(_K0, KV_WIDTH)
    v = proj(_V0, KV_WIDTH)
    row_group = lax.broadcasted_iota(jnp.int32, (Q_PER_KV * ATT_BLOCK, 1), 0) // ATT_BLOCK
    col = lax.broadcasted_iota(jnp.int32, (Q_PER_KV * ATT_BLOCK, 2 * ATT_BLOCK), 1)
    y_att_blocks = []
    for i in range(n_blk):
        rows = slice(i * ATT_BLOCK, (i + 1) * ATT_BLOCK)
        if i == 0:
            k_prev, v_prev = kprev_ref[...], vprev_ref[...]
        else:
            prev = slice((i - 1) * ATT_BLOCK, i * ATT_BLOCK)
            k_prev, v_prev = k[prev], v[prev]
        k_cat = _bf16(jnp.concatenate([k_prev, k[rows]], axis=0))
        v_cat = _bf16(jnp.concatenate([v_prev, v[rows]], axis=0))
        q_blk = _bf16(q[rows])
        heads = [None] * N_Q_HEADS
        for hk in range(N_KV_HEADS):
            kv_cols = slice(hk * HEAD_DIM, (hk + 1) * HEAD_DIM)
            q_stack = jnp.concatenate(
                [q_blk[:, (hk * Q_PER_KV + g) * HEAD_DIM:(hk * Q_PER_KV + g + 1) * HEAD_DIM]
                 for g in range(Q_PER_KV)], axis=0)
            scores = lax.dot_general(q_stack, k_cat[:, kv_cols], (((1,), (1,)), ((), ())),
                                     preferred_element_type=jnp.float32)
            logits = scores * (HEAD_DIM ** -0.5) + abias_ref[hk]
            if i == 0:
                n_masked = jnp.where(j == 0, ATT_BLOCK, 0)
                logits = jnp.where(col < n_masked, _NEG_INF, logits)
            sink = jnp.zeros((Q_PER_KV * ATT_BLOCK, 1), jnp.float32)
            for g in range(Q_PER_KV):
                sink = jnp.where(row_group == g, sinks_ref[hk * Q_PER_KV + g], sink)
            m = jnp.maximum(jnp.max(logits, axis=-1, keepdims=True), sink)
            p = jnp.exp(logits - m)
            probs = p / (jnp.sum(p, axis=-1, keepdims=True) + jnp.exp(sink - m))
            out = _dot(_bf16(probs), v_cat[:, kv_cols])
            for g in range(Q_PER_KV):
                heads[hk * Q_PER_KV + g] = out[g * ATT_BLOCK:(g + 1) * ATT_BLOCK]
        y_att_blocks.append(jnp.concatenate(heads, axis=1))
    kprev_ref[...] = k[tm - ATT_BLOCK:]
    vprev_ref[...] = v[tm - ATT_BLOCK:]
    y_att = jnp.concatenate(y_att_blocks, axis=0)
    branch_att = _dot(_bf16(y_att), wba_ref[...])

    u = jax.nn.gelu(proj(_U0, SG_WIDTH))
    sv = jax.nn.gelu(proj(_SV0, SG_WIDTH))
    mu = jnp.mean(sv, axis=-1, keepdims=True)
    var = jnp.mean(jnp.square(sv - mu), axis=-1, keepdims=True)
    vn = _bf16((sv - mu) * lax.rsqrt(var + NORM_EPS) * lng_ref[...] + lnb_ref[...])
    causal = (lax.broadcasted_iota(jnp.int32, (ATT_BLOCK, ATT_BLOCK), 0)
              >= lax.broadcasted_iota(jnp.int32, (ATT_BLOCK, ATT_BLOCK), 1))
    w_sp = [_bf16(jnp.where(causal, sgw_ref[g], 0.0)) for g in range(N_SG_GROUPS)]
    mixed_chunks = []
    for cidx in range(n_blk):
        rows = slice(cidx * ATT_BLOCK, (cidx + 1) * ATT_BLOCK)
        groups = [_dot(w_sp[g], vn[rows, g * SG_GROUP_DIM:(g + 1) * SG_GROUP_DIM])
                  for g in range(N_SG_GROUPS)]
        mixed_chunks.append(jnp.concatenate(groups, axis=1) + sgb_ref[...])
    y_sg = u * jnp.concatenate(mixed_chunks, axis=0)
    branch_sg = _dot(_bf16(y_sg), wbs_ref[...])

    merged = (jax.nn.sigmoid(proj(_GA0, D_MODEL)) * branch_att
              + jax.nn.sigmoid(proj(_GS0, D_MODEL)) * branch_sg)
    y = _dot(_bf16(merged), wout_ref[...]) + bout_ref[...]
    x1 = x + gate1 * _rms(y, gpost_ref[...])
    x1_ref[0] = x1

    h2 = _rms(x1, gffn_ref[...]) * (1.0 + scale2) + shift2
    for s in range(ROW_TILES):
        h2_ref[pl.ds(s, tm, stride=ROW_TILES), :] = h2[:, s * LANES:(s + 1) * LANES]
    logits = _dot(_bf16(h2), wr_ref[...]) + br_ref[...]
    lane = lax.broadcasted_iota(jnp.int32, (tm, LANES), 1).astype(jnp.float32)
    top_val, top_idx = [], []
    selected = jnp.zeros((tm, LANES), jnp.float32)
    for _ in range(TOP_K):
        mval = jnp.max(logits, axis=-1, keepdims=True)
        midx = jnp.min(jnp.where(logits == mval, lane, float(LANES)), axis=-1, keepdims=True)
        hit = lane == midx
        selected = jnp.where(hit, 1.0, selected)
        logits = jnp.where(hit, _NEG_INF, logits)
        top_val.append(mval)
        top_idx.append(midx)
    expv = [jnp.exp(tv - top_val[0]) for tv in top_val]
    denom = expv[0] + expv[1] + expv[2] + expv[3]
    strict_lower = (lax.broadcasted_iota(jnp.int32, (tm, tm), 0)
                    > lax.broadcasted_iota(jnp.int32, (tm, tm), 1))
    before = _dot(_bf16(jnp.where(strict_lower, 1.0, 0.0)), _bf16(selected)) + carry_ref[0:1, :]
    route_i = jnp.zeros((tm, LANES), jnp.float32)
    route_g = jnp.zeros((tm, LANES), jnp.float32)
    for kk in range(TOP_K):
        rank = jnp.sum(jnp.where(lane == top_idx[kk], before, 0.0), axis=-1, keepdims=True)
        route_i = jnp.where(lane == float(kk), top_idx[kk], route_i)
        route_i = jnp.where(lane == float(TOP_K + kk), rank, route_i)
        route_g = jnp.where(lane == float(kk), expv[kk] / denom, route_g)
    ri_ref[...] = route_i.astype(jnp.int32)
    rg_ref[...] = route_g
    total = carry_ref[0:1, :] + jnp.sum(selected, axis=0, keepdims=True)
    carry_ref[...] = jnp.broadcast_to(total, carry_ref.shape)
    cnt_ref[...] = jnp.broadcast_to(total, cnt_ref.shape)


def _mixer(x, mod, p):
    tm = MIX_TILE
    n_j = SEQ // tm
    const2 = lambda shape: pl.BlockSpec(shape, lambda b, j, s: (0, 0), pipeline_mode=pl.Buffered(1))
    const3 = lambda shape: pl.BlockSpec(shape, lambda b, j, s: (0, 0, 0), pipeline_mode=pl.Buffered(1))
    tile_rows = lambda b, j, s: (b * n_j + j, 0)
    in_specs = [
        pl.BlockSpec((1, tm, D_MODEL), lambda b, j, s: (b, j, 0)),
        pl.BlockSpec((1, 1, 6 * D_MODEL), lambda b, j, s: (b, 0, 0)),
        const2((1, D_MODEL)), const2((1, D_MODEL)),
        const2((D_MODEL, IN_WIDTH)), const2((1, IN_WIDTH)),
        const3((N_KV_HEADS, Q_PER_KV * ATT_BLOCK, 2 * ATT_BLOCK)),
        const2((1, SG_WIDTH)), const2((1, SG_WIDTH)),
        const3((N_SG_GROUPS, ATT_BLOCK, ATT_BLOCK)), const2((ATT_BLOCK, SG_WIDTH)),
        const2((ATTN_WIDTH, D_MODEL)), const2((SG_WIDTH, D_MODEL)),
        const2((D_MODEL, D_MODEL)), const2((1, D_MODEL)),
        const2((1, D_MODEL)),
        const2((D_MODEL, LANES)), const2((1, LANES)),
    ]
    out_specs = [
        pl.BlockSpec((1, tm, D_MODEL), lambda b, j, s: (b, j, 0)),
        pl.BlockSpec((tm * ROW_TILES, LANES), tile_rows),
        pl.BlockSpec((tm, LANES), tile_rows),
        pl.BlockSpec((tm, LANES), tile_rows),
        pl.BlockSpec((SUBLANES, LANES), lambda b, j, s: (0, 0)),
    ]
    out_shape = [
        jax.ShapeDtypeStruct((BATCH, SEQ, D_MODEL), jnp.float32),
        jax.ShapeDtypeStruct((N_TOKENS * ROW_TILES, LANES), jnp.float32),
        jax.ShapeDtypeStruct((N_TOKENS, LANES), jnp.int32),
        jax.ShapeDtypeStruct((N_TOKENS, LANES), jnp.float32),
        jax.ShapeDtypeStruct((SUBLANES, LANES), jnp.float32),
    ]
    return pl.pallas_call(
        _mixer_kernel,
        grid_spec=pltpu.PrefetchScalarGridSpec(
            num_scalar_prefetch=1,
            grid=(BATCH, n_j),
            in_specs=in_specs,
            out_specs=out_specs,
            scratch_shapes=[
                pltpu.VMEM((ATT_BLOCK, KV_WIDTH), jnp.float32),
                pltpu.VMEM((ATT_BLOCK, KV_WIDTH), jnp.float32),
                pltpu.VMEM((SUBLANES, LANES), jnp.float32),
            ]),
        out_shape=out_shape,
        compiler_params=pltpu.CompilerParams(
            dimension_semantics=("arbitrary", "arbitrary"), vmem_limit_bytes=VMEM_LIMIT_BYTES),
        name="mixer_router",
    )(p["sinks"], x, mod.reshape(BATCH, 1, 6 * D_MODEL), p["g_pre_mix"], p["g_post_mix"],
      p["w_in"], p["b_in"], _attention_bias(), p["sg_ln_g"], p["sg_ln_b"], p["sg_w"], p["sg_b"],
      p["w_br_attn"], p["w_br_sg"], p["w_out"], p["b_out"], p["g_pre_ffn"],
      p["w_router"], p["b_router"])


def _row_copy(src, src_row, dst, dst_row, sem):
    return pltpu.make_async_copy(
        src.at[pl.ds(pl.multiple_of(src_row * ROW_TILES, ROW_TILES), ROW_TILES)],
        dst.at[pl.ds(pl.multiple_of(dst_row * ROW_TILES, ROW_TILES), ROW_TILES)], sem)


def _wait_rows(src, dst, n_rows, sem):
    pltpu.make_async_copy(src.at[pl.ds(0, n_rows * ROW_TILES)],
                          dst.at[pl.ds(0, n_rows * ROW_TILES)], sem).wait()


def _rows_to_matrix(rows_ref, first_row, n_rows):
    return jnp.concatenate(
        [rows_ref[pl.ds(first_row * ROW_TILES + s, n_rows, stride=ROW_TILES), :]
         for s in range(ROW_TILES)], axis=1)


def _dispatch_kernel(slot_ref, pad_ref, h2_ref, xs_hbm, sem):
    def issue(g, carry):
        for r in range(DMA_UNROLL):
            a = g * DMA_UNROLL + r
            tok = g * (DMA_UNROLL // TOP_K) + r // TOP_K
            _row_copy(h2_ref, tok, xs_hbm, slot_ref[0, 0, a], sem).start(priority=r % 2)
        return carry

    lax.fori_loop(0, DSP_TILE * TOP_K // DMA_UNROLL, issue, 0)

    def issue_pad(g, carry):
        for r in range(DMA_UNROLL):
            _row_copy(h2_ref, 0, xs_hbm, pad_ref[0, 0, g * DMA_UNROLL + r], sem).start(priority=r % 2)
        return carry

    lax.fori_loop(0, DSP_PADS // DMA_UNROLL, issue_pad, 0)
    n_rows = (DSP_TILE * TOP_K + DSP_PADS) * ROW_TILES
    pltpu.make_async_copy(xs_hbm.at[pl.ds(0, n_rows)], xs_hbm.at[pl.ds(n_rows, n_rows)], sem).wait()


def _dispatch(slot, pad_slots, h2_rows):
    n_steps = N_TOKENS // DSP_TILE
    return pl.pallas_call(
        _dispatch_kernel,
        grid=(n_steps,),
        in_specs=[
            pl.BlockSpec((1, 1, DSP_TILE * TOP_K), lambda i: (i, 0, 0), memory_space=pltpu.SMEM),
            pl.BlockSpec((1, 1, DSP_PADS), lambda i: (i, 0, 0), memory_space=pltpu.SMEM),
            pl.BlockSpec((DSP_TILE * ROW_TILES, LANES), lambda i: (i, 0)),
        ],
        out_specs=pl.BlockSpec(memory_space=pl.ANY),
        out_shape=jax.ShapeDtypeStruct((N_SLOTS * ROW_TILES, LANES), jnp.float32),
        scratch_shapes=[pltpu.SemaphoreType.DMA],
        compiler_params=pltpu.CompilerParams(dimension_semantics=("arbitrary",)),
        name="dispatch",
    )(slot.reshape(n_steps, 1, DSP_TILE * TOP_K), pad_slots.reshape(n_steps, 1, DSP_PADS), h2_rows)


def _expert_kernel(be_ref, xs_ref, w1_ref, b1_ref, w2_ref, b2_ref, ys_ref):
    del be_ref
    xb = _bf16(_rows_to_matrix(xs_ref, 0, MOE_BLOCK))
    hb = _dot(xb, w1_ref[0]) + b1_ref[0]
    g = jnp.minimum(hb[:, :D_EXPERT], SWIGLU_LIMIT)
    lin = jnp.clip(hb[:, D_EXPERT:], -SWIGLU_LIMIT, SWIGLU_LIMIT)
    act = g * jax.nn.sigmoid(SWIGLU_ALPHA * g) * (lin + 1.0)
    y = _dot(_bf16(act), w2_ref[0]) + b2_ref[0]
    for s in range(ROW_TILES):
        ys_ref[pl.ds(s, MOE_BLOCK, stride=ROW_TILES), :] = y[:, s * LANES:(s + 1) * LANES]


def _experts(block_e, xs_rows, w1, b1, w2, b2):
    return pl.pallas_call(
        _expert_kernel,
        grid_spec=pltpu.PrefetchScalarGridSpec(
            num_scalar_prefetch=1,
            grid=(N_MOE_BLOCKS,),
            in_specs=[
                pl.BlockSpec((MOE_BLOCK * ROW_TILES, LANES), lambda i, be: (i, 0)),
                pl.BlockSpec((1, D_MODEL, 2 * D_EXPERT), lambda i, be: (be[i], 0, 0)),
                pl.BlockSpec((1, 1, 2 * D_EXPERT), lambda i, be: (be[i], 0, 0)),
                pl.BlockSpec((1, D_EXPERT, D_MODEL), lambda i, be: (be[i], 0, 0)),
                pl.BlockSpec((1, 1, D_MODEL), lambda i, be: (be[i], 0, 0)),
            ],
            out_specs=pl.BlockSpec((MOE_BLOCK * ROW_TILES, LANES), lambda i, be: (i, 0)),
        ),
        out_shape=jax.ShapeDtypeStruct((N_SLOTS * ROW_TILES, LANES), jnp.float32),
        compiler_params=pltpu.CompilerParams(
            dimension_semantics=("arbitrary",), vmem_limit_bytes=VMEM_LIMIT_BYTES),
        name="experts",
    )(block_e, xs_rows,
      w1, b1.reshape(N_EXPERTS, 1, 2 * D_EXPERT), w2, b2.reshape(N_EXPERTS, 1, D_MODEL))


def _combine_kernel(slot0_ref, slot_next_ref, ys_hbm, x1_ref, rg_ref, mod_ref, gpost_ref, o_ref,
                    buf_ref, sems):
    tm = CMB_TILE
    n_rows = TOP_K * tm
    i = pl.program_id(0)
    n_steps = pl.num_programs(0)

    def gather(slot_ref, half):
        def issue(g, carry):
            for r in range(DMA_UNROLL):
                a = g * DMA_UNROLL + r
                _row_copy(ys_hbm, slot_ref[0, 0, a], buf_ref.at[half], a, sems.at[half]).start(
                    priority=r % 2)
            return carry
        lax.fori_loop(0, n_rows // DMA_UNROLL, issue, 0)

    @pl.when(i == 0)
    def _():
        gather(slot0_ref, 0)

    @pl.when(i + 1 < n_steps)
    def _():
        gather(slot_next_ref, (i + 1) % 2)

    cur = i % 2
    _wait_rows(ys_hbm, buf_ref.at[cur], n_rows, sems.at[cur])
    gates = rg_ref[...]
    rows = buf_ref.at[cur]
    y = jnp.zeros((tm, D_MODEL), jnp.float32)
    for kk in range(TOP_K):
        y = y + _rows_to_matrix(rows, kk * tm, tm) * gates[:, kk:kk + 1]
    gate2 = mod_ref[0][:, 5 * D_MODEL:6 * D_MODEL]
    o_ref[...] = x1_ref[...] + gate2 * _rms(y, gpost_ref[...])


def _combine(slot_kmajor, ys_rows, x1, route_g, mod, g_post_ffn):
    tm = CMB_TILE
    n_tiles = N_TOKENS // tm
    tiles_per_seq = SEQ // tm
    return pl.pallas_call(
        _combine_kernel,
        grid=(n_tiles,),
        in_specs=[
            pl.BlockSpec((1, 1, TOP_K * tm), lambda i: (0, 0, 0), memory_space=pltpu.SMEM),
            pl.BlockSpec((1, 1, TOP_K * tm), lambda i: (jnp.minimum(i + 1, n_tiles - 1), 0, 0),
                         memory_space=pltpu.SMEM),
            pl.BlockSpec(memory_space=pl.ANY),
            pl.BlockSpec((tm, D_MODEL), lambda i: (i, 0)),
            pl.BlockSpec((tm, LANES), lambda i: (i, 0)),
            pl.BlockSpec((1, 1, 6 * D_MODEL), lambda i: (i // tiles_per_seq, 0, 0)),
            pl.BlockSpec((1, D_MODEL), lambda i: (0, 0)),
        ],
        out_specs=pl.BlockSpec((tm, D_MODEL), lambda i: (i, 0)),
        out_shape=jax.ShapeDtypeStruct((N_TOKENS, D_MODEL), jnp.float32),
        scratch_shapes=[
            pltpu.VMEM((2, TOP_K * tm * ROW_TILES, LANES), jnp.float32),
            pltpu.SemaphoreType.DMA((2,)),
        ],
        compiler_params=pltpu.CompilerParams(
            dimension_semantics=("arbitrary",), vmem_limit_bytes=VMEM_LIMIT_BYTES),
        name="combine",
    )(slot_kmajor, slot_kmajor, ys_rows, x1, route_g, mod.reshape(BATCH, 1, 6 * D_MODEL),
      g_post_ffn)


def kernel(x, c, w_mod, b_mod, g_pre_mix, g_post_mix, w_in, b_in, attn_sinks, sg_ln_g, sg_ln_b,
           sg_w, sg_b, w_br_attn, w_br_sg, w_out, b_out, g_pre_ffn, g_post_ffn, w_router,
           b_router, w_mlp1, b_mlp1, w_mlp2, b_mlp2):
    assert x.shape == (BATCH, SEQ, D_MODEL) and w_mod.shape[0] == 1, "single-layer shapes only"
    row = lambda a: a[0].reshape(1, -1)
    bf = lambda a: a[0].astype(jnp.bfloat16)

    mod = _modulation(c, w_mod[0], b_mod[0])

    params = dict(
        sinks=attn_sinks[0], g_pre_mix=row(g_pre_mix), g_post_mix=row(g_post_mix),
        w_in=bf(w_in), b_in=row(b_in), sg_ln_g=row(sg_ln_g), sg_ln_b=row(sg_ln_b), sg_w=sg_w[0],
        sg_b=jnp.repeat(sg_b[0].T, SG_GROUP_DIM, axis=1),
        w_br_attn=bf(w_br_attn), w_br_sg=bf(w_br_sg), w_out=bf(w_out), b_out=row(b_out),
        g_pre_ffn=row(g_pre_ffn),
        w_router=jnp.pad(bf(w_router), ((0, 0), (0, LANES - N_EXPERTS))),
        b_router=jnp.pad(row(b_router), ((0, 0), (0, LANES - N_EXPERTS)), constant_values=_NEG_INF),
    )
    x1, h2_rows, route_i, route_g, counts = _mixer(x, mod, params)

    counts = counts[0, :N_EXPERTS].astype(jnp.int32)
    padded = (counts + MOE_BLOCK - 1) // MOE_BLOCK * MOE_BLOCK
    pends = jnp.cumsum(padded)
    pstarts = pends - padded
    block_start = jnp.arange(N_MOE_BLOCKS, dtype=jnp.int32) * MOE_BLOCK
    block_e = jnp.minimum(jnp.sum((pends[None, :] <= block_start[:, None]).astype(jnp.int32), axis=1),
                          N_EXPERTS - 1)
    expert_ids = jnp.arange(N_EXPERTS, dtype=jnp.int32)
    top_idx = route_i[:, :TOP_K]
    slot = (jnp.sum(jnp.where(top_idx[:, :, None] == expert_ids, pstarts, 0), axis=-1)
            + route_i[:, TOP_K:2 * TOP_K])
    free_start = jnp.concatenate([pstarts + counts, pends[-1:]])
    free_len = jnp.concatenate([padded - counts, N_SLOTS - pends[-1:]])
    free_end = jnp.cumsum(free_len)
    q = jnp.arange(N_PAD_SLOTS, dtype=jnp.int32)
    region = jnp.sum((free_end[None, :] <= q[:, None]).astype(jnp.int32), axis=1)
    region_hit = region[:, None] == jnp.arange(N_EXPERTS + 1, dtype=jnp.int32)
    pad_slots = q + jnp.sum(jnp.where(region_hit, free_start - (free_end - free_len), 0), axis=1)

    xs_rows = _dispatch(slot, pad_slots, h2_rows)
    ys_rows = _experts(block_e, xs_rows, bf(w_mlp1), b_mlp1[0], bf(w_mlp2), b_mlp2[0])

    n_tiles = N_TOKENS // CMB_TILE
    slot_kmajor = slot.reshape(n_tiles, CMB_TILE, TOP_K).transpose(0, 2, 1).reshape(
        n_tiles, 1, TOP_K * CMB_TILE)
    out = _combine(slot_kmajor, ys_rows, x1.reshape(N_TOKENS, D_MODEL), route_g, mod, row(g_post_ffn))
    return out.reshape(BATCH, SEQ, D_MODEL)
```

```python
import functools

import jax
import jax.numpy as jnp
import numpy as np
from jax import lax
from jax.experimental import pallas as pl
from jax.experimental.pallas import tpu as pltpu

D_MODEL = 1024
BATCH = 16
SEQ = 2048
N_TOKENS = BATCH * SEQ
HEAD_DIM = 64
N_Q_HEADS = 8
N_KV_HEADS = 2
Q_PER_KV = N_Q_HEADS // N_KV_HEADS
ATTN_WIDTH = N_Q_HEADS * HEAD_DIM
KV_WIDTH = N_KV_HEADS * HEAD_DIM
ATT_BLOCK = 128
N_SG_GROUPS = 8
SG_GROUP_DIM = 64
SG_WIDTH = N_SG_GROUPS * SG_GROUP_DIM
N_EXPERTS = 32
TOP_K = 4
D_EXPERT = D_MODEL
SWIGLU_LIMIT = 7.0
SWIGLU_ALPHA = 1.702
MOE_BLOCK = 512
NORM_EPS = 1e-6
N_ASSIGN = N_TOKENS * TOP_K
N_MOE_BLOCKS = N_ASSIGN // MOE_BLOCK + N_EXPERTS
N_SLOTS = N_MOE_BLOCKS * MOE_BLOCK

_Q0 = 0
_K0 = _Q0 + ATTN_WIDTH
_V0 = _K0 + KV_WIDTH
_U0 = _V0 + KV_WIDTH
_SV0 = _U0 + SG_WIDTH
_GA0 = _SV0 + SG_WIDTH
_GS0 = _GA0 + D_MODEL
IN_WIDTH = _GS0 + D_MODEL

LANES = 128
SUBLANES = 8
ROW_TILES = D_MODEL // LANES
assert ROW_TILES == SUBLANES
VMEM_LIMIT_BYTES = 56 * 1024 * 1024

MIX_TILE = 512
CMB_TILE = 256
DSP_TILE = 256
DMA_UNROLL = 8
N_PAD_SLOTS = N_SLOTS - N_ASSIGN
DSP_PADS = N_PAD_SLOTS // (N_TOKENS // DSP_TILE)
assert DMA_UNROLL % TOP_K == 0 and DSP_PADS % DMA_UNROLL == 0

_NEG_INF = float("-inf")


def _rms(x, g):
    return x * lax.rsqrt(jnp.mean(x * x, axis=-1, keepdims=True) + NORM_EPS) * g


def _bf16(x):
    return x.astype(jnp.bfloat16)


def _dot(a, b):
    return jnp.dot(a, b, preferred_element_type=jnp.float32)


def _mod_kernel(c_ref, w_ref, b_ref, o_ref):
    c = c_ref[...]
    c_act = c * jax.nn.sigmoid(c)
    o_ref[...] = _dot(_bf16(c_act), _bf16(w_ref[...])) + b_ref[...]


def _modulation(c, w_mod, b_mod):
    n_out = w_mod.shape[1]
    tn = D_MODEL
    return pl.pallas_call(
        _mod_kernel,
        grid=(n_out // tn,),
        in_specs=[
            pl.BlockSpec((BATCH, D_MODEL), lambda n: (0, 0)),
            pl.BlockSpec((D_MODEL, tn), lambda n: (0, n)),
            pl.BlockSpec((1, tn), lambda n: (0, n)),
        ],
        out_specs=pl.BlockSpec((BATCH, tn), lambda n: (0, n)),
        out_shape=jax.ShapeDtypeStruct((BATCH, n_out), jnp.float32),
        compiler_params=pltpu.CompilerParams(dimension_semantics=("arbitrary",)),
        name="modulation",
    )(c, w_mod, b_mod.reshape(1, n_out))


def _attention_bias():
    slopes = 2.0 ** (-8.0 * np.arange(1, N_Q_HEADS + 1) / N_Q_HEADS)
    qi = np.arange(ATT_BLOCK)[:, None]
    kj = np.arange(2 * ATT_BLOCK)[None, :]
    dist = qi + ATT_BLOCK - kj
    in_win = (dist >= 0) & (dist < ATT_BLOCK)
    bias = np.where(in_win[None], -slopes[:, None, None] * dist[None].astype(np.float64), -np.inf)
    bias = bias.reshape(N_KV_HEADS, Q_PER_KV * ATT_BLOCK, 2 * ATT_BLOCK)
    return jnp.asarray(bias, dtype=jnp.float32)


def _mixer_kernel(sinks_ref, x_ref, mod_ref, gpre_ref, gpost_ref, win_ref, bin_ref, abias_ref,
                  lng_ref, lnb_ref, sgw_ref, sgb_ref, wba_ref, wbs_ref, wout_ref, bout_ref,
                  gffn_ref, wr_ref, br_ref,
                  x1_ref, h2_ref, ri_ref, rg_ref, cnt_ref,
                  kprev_ref, vprev_ref, carry_ref):
    b = pl.program_id(0)
    j = pl.program_id(1)
    tm = MIX_TILE
    n_blk = tm // ATT_BLOCK

    @pl.when((b == 0) & (j == 0))
    def _():
        carry_ref[...] = jnp.zeros_like(carry_ref)

    @pl.when(j == 0)
    def _():
        kprev_ref[...] = jnp.zeros_like(kprev_ref)
        vprev_ref[...] = jnp.zeros_like(vprev_ref)

    x = x_ref[0]
    mod = mod_ref[0]
    shift1, scale1, gate1, shift2, scale2, gate2 = (
        mod[:, i * D_MODEL:(i + 1) * D_MODEL] for i in range(6))

    h = _bf16(_rms(x, gpre_ref[...]) * (1.0 + scale1) + shift1)

    def proj(lo, width):
        return _dot(h, win_ref[:, lo:lo + width]) + bin_ref[:, lo:lo + width]

    q = proj(_Q0, ATTN_WIDTH)
    k = proj(_K0, KV_WIDTH)
    v = proj(_V0, KV_WIDTH)
    row_group = lax.broadcasted_iota(jnp.int32, (Q_PER_KV * ATT_BLOCK, 1), 0) // ATT_BLOCK
    col = lax.broadcasted_iota(jnp.int32, (Q_PER_KV * ATT_BLOCK, 2 * ATT_BLOCK), 1)
    y_att_blocks = []
    for i in range(n_blk):
        rows = slice(i * ATT_BLOCK, (i + 1) * ATT_BLOCK)
        if i == 0:
            k_prev, v_prev = kprev_ref[...], vprev_ref[...]
        else:
            prev = slice((i - 1) * ATT_BLOCK, i * ATT_BLOCK)
            k_prev, v_prev = k[prev], v[prev]
        k_cat = _bf16(jnp.concatenate([k_prev, k[rows]], axis=0))
        v_cat = _bf16(jnp.concatenate([v_prev, v[rows]], axis=0))
        q_blk = _bf16(q[rows])
        heads = [None] * N_Q_HEADS
        for hk in range(N_KV_HEADS):
            kv_cols = slice(hk * HEAD_DIM, (hk + 1) * HEAD_DIM)
            q_stack = jnp.concatenate(
                [q_blk[:, (hk * Q_PER_KV + g) * HEAD_DIM:(hk * Q_PER_KV + g + 1) * HEAD_DIM]
                 for g in range(Q_PER_KV)], axis=0)
            scores = lax.dot_general(q_stack, k_cat[:, kv_cols], (((1,), (1,)), ((), ())),
                                     preferred_element_type=jnp.float32)
            logits = scores * (HEAD_DIM ** -0.5) + abias_ref[hk]
            if i == 0:
                n_masked = jnp.where(j == 0, ATT_BLOCK, 0)
                logits = jnp.where(col < n_masked, _NEG_INF, logits)
            sink = jnp.zeros((Q_PER_KV * ATT_BLOCK, 1), jnp.float32)
            for g in range(Q_PER_KV):
                sink = jnp.where(row_group == g, sinks_ref[hk * Q_PER_KV + g], sink)
            m = jnp.maximum(jnp.max(logits, axis=-1, keepdims=True), sink)
            p = jnp.exp(logits - m)
            probs = p / (jnp.sum(p, axis=-1, keepdims=True) + jnp.exp(sink - m))
            out = _dot(_bf16(probs), v_cat[:, kv_cols])
            for g in range(Q_PER_KV):
                heads[hk * Q_PER_KV + g] = out[g * ATT_BLOCK:(g + 1) * ATT_BLOCK]
        y_att_blocks.append(jnp.concatenate(heads, axis=1))
    kprev_ref[...] = k[tm - ATT_BLOCK:]
    vprev_ref[...] = v[tm - ATT_BLOCK:]
    y_att = jnp.concatenate(y_att_blocks, axis=0)
    branch_att = _dot(_bf16(y_att), wba_ref[...])

    u = jax.nn.gelu(proj(_U0, SG_WIDTH))
    sv = jax.nn.gelu(proj(_SV0, SG_WIDTH))
    mu = jnp.mean(sv, axis=-1, keepdims=True)
    var = jnp.mean(jnp.square(sv - mu), axis=-1, keepdims=True)
    vn = _bf16((sv - mu) * lax.rsqrt(var + NORM_EPS) * lng_ref[...] + lnb_ref[...])
    causal = (lax.broadcasted_iota(jnp.int32, (ATT_BLOCK, ATT_BLOCK), 0)
              >= lax.broadcasted_iota(jnp.int32, (ATT_BLOCK, ATT_BLOCK), 1))
    w_sp = [_bf16(jnp.where(causal, sgw_ref[g], 0.0)) for g in range(N_SG_GROUPS)]
    mixed_chunks = []
    for cidx in range(n_blk):
        rows = slice(cidx * ATT_BLOCK, (cidx + 1) * ATT_BLOCK)
        groups = [_dot(w_sp[g], vn[rows, g * SG_GROUP_DIM:(g + 1) * SG_GROUP_DIM])
                  for g in range(N_SG_GROUPS)]
        mixed_chunks.append(jnp.concatenate(groups, axis=1) + sgb_ref[...])
    y_sg = u * jnp.concatenate(mixed_chunks, axis=0)
    branch_sg = _dot(_bf16(y_sg), wbs_ref[...])

    merged = (jax.nn.sigmoid(proj(_GA0, D_MODEL)) * branch_att
              + jax.nn.sigmoid(proj(_GS0, D_MODEL)) * branch_sg)
    y = _dot(_bf16(merged), wout_ref[...]) + bout_ref[...]
    x1 = x + gate1 * _rms(y, gpost_ref[...])
    x1_ref[0] = x1

    h2 = _rms(x1, gffn_ref[...]) * (1.0 + scale2) + shift2
    for s in range(ROW_TILES):
        h2_ref[pl.ds(s, tm, stride=ROW_TILES), :] = h2[:, s * LANES:(s + 1) * LANES]
    logits = _dot(_bf16(h2), wr_ref[...]) + br_ref[...]
    lane = lax.broadcasted_iota(jnp.int32, (tm, LANES), 1).astype(jnp.float32)
    top_val, top_idx = [], []
    selected = jnp.zeros((tm, LANES), jnp.float32)
    for _ in range(TOP_K):
        mval = jnp.max(logits, axis=-1, keepdims=True)
        midx = jnp.min(jnp.where(logits == mval, lane, float(LANES)), axis=-1, keepdims=True)
        hit = lane == midx
        selected = jnp.where(hit, 1.0, selected)
        logits = jnp.where(hit, _NEG_INF, logits)
        top_val.append(mval)
        top_idx.append(midx)
    expv = [jnp.exp(tv - top_val[0]) for tv in top_val]
    denom = expv[0] + expv[1] + expv[2] + expv[3]
    strict_lower = (lax.broadcasted_iota(jnp.int32, (tm, tm), 0)
                    > lax.broadcasted_iota(jnp.int32, (tm, tm), 1))
    before = _dot(_bf16(jnp.where(strict_lower, 1.0, 0.0)), _bf16(selected)) + carry_ref[0:1, :]
    route_i = jnp.zeros((tm, LANES), jnp.float32)
    route_g = jnp.zeros((tm, LANES), jnp.float32)
    for kk in range(TOP_K):
        rank = jnp.sum(jnp.where(lane == top_idx[kk], before, 0.0), axis=-1, keepdims=True)
        route_i = jnp.where(lane == float(kk), top_idx[kk], route_i)
        route_i = jnp.where(lane == float(TOP_K + kk), rank, route_i)
        route_g = jnp.where(lane == float(kk), expv[kk] / denom, route_g)
    ri_ref[...] = route_i.astype(jnp.int32)
    rg_ref[...] = route_g
    total = carry_ref[0:1, :] + jnp.sum(selected, axis=0, keepdims=True)
    carry_ref[...] = jnp.broadcast_to(total, carry_ref.shape)
    cnt_ref[...] = jnp.broadcast_to(total, cnt_ref.shape)


def _mixer(x, mod, p):
    tm = MIX_TILE
    n_j = SEQ // tm
    const2 = lambda shape: pl.BlockSpec(shape, lambda b, j, s: (0, 0), pipeline_mode=pl.Buffered(1))
    const3 = lambda shape: pl.BlockSpec(shape, lambda b, j, s: (0, 0, 0), pipeline_mode=pl.Buffered(1))
    tile_rows = lambda b, j, s: (b * n_j + j, 0)
    in_specs = [
        pl.BlockSpec((1, tm, D_MODEL), lambda b, j, s: (b, j, 0)),
        pl.BlockSpec((1, 1, 6 * D_MODEL), lambda b, j, s: (b, 0, 0)),
        const2((1, D_MODEL)), const2((1, D_MODEL)),
        const2((D_MODEL, IN_WIDTH)), const2((1, IN_WIDTH)),
        const3((N_KV_HEADS, Q_PER_KV * ATT_BLOCK, 2 * ATT_BLOCK)),
        const2((1, SG_WIDTH)), const2((1, SG_WIDTH)),
        const3((N_SG_GROUPS, ATT_BLOCK, ATT_BLOCK)), const2((ATT_BLOCK, SG_WIDTH)),
        const2((ATTN_WIDTH, D_MODEL)), const2((SG_WIDTH, D_MODEL)),
        const2((D_MODEL, D_MODEL)), const2((1, D_MODEL)),
        const2((1, D_MODEL)),
        const2((D_MODEL, LANES)), const2((1, LANES)),
    ]
    out_specs = [
        pl.BlockSpec((1, tm, D_MODEL), lambda b, j, s: (b, j, 0)),
        pl.BlockSpec((tm * ROW_TILES, LANES), tile_rows),
        pl.BlockSpec((tm, LANES), tile_rows),
        pl.BlockSpec((tm, LANES), tile_rows),
        pl.BlockSpec((SUBLANES, LANES), lambda b, j, s: (0, 0)),
    ]
    out_shape = [
        jax.ShapeDtypeStruct((BATCH, SEQ, D_MODEL), jnp.float32),
        jax.ShapeDtypeStruct((N_TOKENS * ROW_TILES, LANES), jnp.float32),
        jax.ShapeDtypeStruct((N_TOKENS, LANES), jnp.int32),
        jax.ShapeDtypeStruct((N_TOKENS, LANES), jnp.float32),
        jax.ShapeDtypeStruct((SUBLANES, LANES), jnp.float32),
    ]
    return pl.pallas_call(
        _mixer_kernel,
        grid_spec=pltpu.PrefetchScalarGridSpec(
            num_scalar_prefetch=1,
            grid=(BATCH, n_j),
            in_specs=in_specs,
            out_specs=out_specs,
            scratch_shapes=[
                pltpu.VMEM((ATT_BLOCK, KV_WIDTH), jnp.float32),
                pltpu.VMEM((ATT_BLOCK, KV_WIDTH), jnp.float32),
                pltpu.VMEM((SUBLANES, LANES), jnp.float32),
            ]),
        out_shape=out_shape,
        compiler_params=pltpu.CompilerParams(
            dimension_semantics=("arbitrary", "arbitrary"), vmem_limit_bytes=VMEM_LIMIT_BYTES),
        name="mixer_router",
    )(p["sinks"], x, mod.reshape(BATCH, 1, 6 * D_MODEL), p["g_pre_mix"], p["g_post_mix"],
      p["w_in"], p["b_in"], _attention_bias(), p["sg_ln_g"], p["sg_ln_b"], p["sg_w"], p["sg_b"],
      p["w_br_attn"], p["w_br_sg"], p["w_out"], p["b_out"], p["g_pre_ffn"],
      p["w_router"], p["b_router"])


def _row_copy(src, src_row, dst, dst_row, sem):
    return pltpu.make_async_copy(
        src.at[pl.ds(pl.multiple_of(src_row * ROW_TILES, ROW_TILES), ROW_TILES)],
        dst.at[pl.ds(pl.multiple_of(dst_row * ROW_TILES, ROW_TILES), ROW_TILES)], sem)


def _wait_rows(src, dst, n_rows, sem):
    pltpu.make_async_copy(src.at[pl.ds(0, n_rows * ROW_TILES)],
                          dst.at[pl.ds(0, n_rows * ROW_TILES)], sem).wait()


def _rows_to_matrix(rows_ref, first_row, n_rows):
    return jnp.concatenate(
        [rows_ref[pl.ds(first_row * ROW_TILES + s, n_rows, stride=ROW_TILES), :]
         for s in range(ROW_TILES)], axis=1)


def _dispatch_kernel(slot_ref, pad_ref, h2_ref, xs_hbm, sem):
    def issue(g, carry):
        for r in range(DMA_UNROLL):
            a = g * DMA_UNROLL + r
            tok = g * (DMA_UNROLL // TOP_K) + r // TOP_K
            _row_copy(h2_ref, tok, xs_hbm, slot_ref[0, 0, a], sem).start(priority=r % 2)
        return carry

    lax.fori_loop(0, DSP_TILE * TOP_K // DMA_UNROLL, issue, 0)

    def issue_pad(g, carry):
        for r in range(DMA_UNROLL):
            _row_copy(h2_ref, 0, xs_hbm, pad_ref[0, 0, g * DMA_UNROLL + r], sem).start(priority=r % 2)
        return carry

    lax.fori_loop(0, DSP_PADS // DMA_UNROLL, issue_pad, 0)
    n_rows = (DSP_TILE * TOP_K + DSP_PADS) * ROW_TILES
    pltpu.make_async_copy(xs_hbm.at[pl.ds(0, n_rows)], xs_hbm.at[pl.ds(n_rows, n_rows)], sem).wait()


def _dispatch(slot, pad_slots, h2_rows):
    n_steps = N_TOKENS // DSP_TILE
    return pl.pallas_call(
        _dispatch_kernel,
        grid=(n_steps,),
        in_specs=[
            pl.BlockSpec((1, 1, DSP_TILE * TOP_K), lambda i: (i, 0, 0), memory_space=pltpu.SMEM),
            pl.BlockSpec((1, 1, DSP_PADS), lambda i: (i, 0, 0), memory_space=pltpu.SMEM),
            pl.BlockSpec((DSP_TILE * ROW_TILES, LANES), lambda i: (i, 0)),
        ],
        out_specs=pl.BlockSpec(memory_space=pl.ANY),
        out_shape=jax.ShapeDtypeStruct((N_SLOTS * ROW_TILES, LANES), jnp.float32),
        scratch_shapes=[pltpu.SemaphoreType.DMA],
        compiler_params=pltpu.CompilerParams(dimension_semantics=("arbitrary",)),
        name="dispatch",
    )(slot.reshape(n_steps, 1, DSP_TILE * TOP_K), pad_slots.reshape(n_steps, 1, DSP_PADS), h2_rows)


def _expert_kernel(be_ref, nused_ref, xs_ref, w1_ref, b1_ref, w2_ref, b2_ref, ys_ref, w1b_ref, w2b_ref):
    i = pl.program_id(0)
    used = i < nused_ref[0]

    @pl.when(used & ((i == 0) | (be_ref[i] != be_ref[jnp.maximum(i - 1, 0)])))
    def _():
        w1b_ref[...] = _bf16(w1_ref[0])
        w2b_ref[...] = _bf16(w2_ref[0])

    @pl.when(used)
    def _():
        xb = _bf16(_rows_to_matrix(xs_ref, 0, MOE_BLOCK))
        hb = _dot(xb, w1b_ref[...]) + b1_ref[0]
        g = jnp.minimum(hb[:, :D_EXPERT], SWIGLU_LIMIT)
        lin = jnp.clip(hb[:, D_EXPERT:], -SWIGLU_LIMIT, SWIGLU_LIMIT)
        act = g * jax.nn.sigmoid(SWIGLU_ALPHA * g) * (lin + 1.0)
        y = _dot(_bf16(act), w2b_ref[...]) + b2_ref[0]
        for s in range(ROW_TILES):
            ys_ref[pl.ds(s, MOE_BLOCK, stride=ROW_TILES), :] = y[:, s * LANES:(s + 1) * LANES]

    @pl.when(jnp.logical_not(used))
    def _():
        ys_ref[...] = jnp.zeros_like(ys_ref)


def _experts(block_e, n_used, xs_rows, w1, b1, w2, b2):
    return pl.pallas_call(
        _expert_kernel,
        grid_spec=pltpu.PrefetchScalarGridSpec(
            num_scalar_prefetch=2,
            grid=(N_MOE_BLOCKS,),
            in_specs=[
                pl.BlockSpec((MOE_BLOCK * ROW_TILES, LANES), lambda i, be, nu: (i, 0)),
                pl.BlockSpec((1, D_MODEL, 2 * D_EXPERT), lambda i, be, nu: (be[i], 0, 0)),
                pl.BlockSpec((1, 1, 2 * D_EXPERT), lambda i, be, nu: (be[i], 0, 0)),
                pl.BlockSpec((1, D_EXPERT, D_MODEL), lambda i, be, nu: (be[i], 0, 0)),
                pl.BlockSpec((1, 1, D_MODEL), lambda i, be, nu: (be[i], 0, 0)),
            ],
            out_specs=pl.BlockSpec((MOE_BLOCK * ROW_TILES, LANES), lambda i, be, nu: (i, 0)),
            scratch_shapes=[
                pltpu.VMEM((D_MODEL, 2 * D_EXPERT), jnp.bfloat16),
                pltpu.VMEM((D_EXPERT, D_MODEL), jnp.bfloat16),
            ],
        ),
        out_shape=jax.ShapeDtypeStruct((N_SLOTS * ROW_TILES, LANES), jnp.float32),
        compiler_params=pltpu.CompilerParams(
            dimension_semantics=("arbitrary",), vmem_limit_bytes=VMEM_LIMIT_BYTES),
        name="experts",
    )(block_e, n_used, xs_rows,
      w1, b1.reshape(N_EXPERTS, 1, 2 * D_EXPERT), w2, b2.reshape(N_EXPERTS, 1, D_MODEL))


def _combine_kernel(slot0_ref, slot_next_ref, ys_hbm, x1_ref, rg_ref, mod_ref, gpost_ref, o_ref,
                    buf_ref, sems):
    tm = CMB_TILE
    n_rows = TOP_K * tm
    i = pl.program_id(0)
    n_steps = pl.num_programs(0)

    def gather(slot_ref, half):
        def issue(g, carry):
            for r in range(DMA_UNROLL):
                tok = g * (DMA_UNROLL // TOP_K) + r // TOP_K
                _row_copy(ys_hbm, slot_ref[0, 0, g * DMA_UNROLL + r], buf_ref.at[half],
                          (r % TOP_K) * tm + tok, sems.at[half]).start(priority=r % 2)
            return carry
        lax.fori_loop(0, n_rows // DMA_UNROLL, issue, 0)

    @pl.when(i == 0)
    def _():
        gather(slot0_ref, 0)

    @pl.when(i + 1 < n_steps)
    def _():
        gather(slot_next_ref, (i + 1) % 2)

    cur = i % 2
    _wait_rows(ys_hbm, buf_ref.at[cur], n_rows, sems.at[cur])
    gates = rg_ref[...]
    rows = buf_ref.at[cur]
    y = jnp.zeros((tm, D_MODEL), jnp.float32)
    for kk in range(TOP_K):
        y = y + _rows_to_matrix(rows, kk * tm, tm) * gates[:, kk:kk + 1]
    gate2 = mod_ref[0][:, 5 * D_MODEL:6 * D_MODEL]
    o_ref[...] = x1_ref[...] + gate2 * _rms(y, gpost_ref[...])


def _combine(slot_tiles, ys_rows, x1, route_g, mod, g_post_ffn):
    tm = CMB_TILE
    n_tiles = N_TOKENS // tm
    tiles_per_seq = SEQ // tm
    return pl.pallas_call(
        _combine_kernel,
        grid=(n_tiles,),
        in_specs=[
            pl.BlockSpec((1, 1, TOP_K * tm), lambda i: (0, 0, 0), memory_space=pltpu.SMEM),
            pl.BlockSpec((1, 1, TOP_K * tm), lambda i: (jnp.minimum(i + 1, n_tiles - 1), 0, 0),
                         memory_space=pltpu.SMEM),
            pl.BlockSpec(memory_space=pl.ANY),
            pl.BlockSpec((tm, D_MODEL), lambda i: (i, 0)),
            pl.BlockSpec((tm, LANES), lambda i: (i, 0)),
            pl.BlockSpec((1, 1, 6 * D_MODEL), lambda i: (i // tiles_per_seq, 0, 0)),
            pl.BlockSpec((1, D_MODEL), lambda i: (0, 0)),
        ],
        out_specs=pl.BlockSpec((tm, D_MODEL), lambda i: (i, 0)),
        out_shape=jax.ShapeDtypeStruct((N_TOKENS, D_MODEL), jnp.float32),
        scratch_shapes=[
            pltpu.VMEM((2, TOP_K * tm * ROW_TILES, LANES), jnp.float32),
            pltpu.SemaphoreType.DMA((2,)),
        ],
        compiler_params=pltpu.CompilerParams(
            dimension_semantics=("arbitrary",), vmem_limit_bytes=VMEM_LIMIT_BYTES),
        name="combine",
    )(slot_tiles, slot_tiles, ys_rows, x1, route_g, mod.reshape(BATCH, 1, 6 * D_MODEL),
      g_post_ffn)


def kernel(x, c, w_mod, b_mod, g_pre_mix, g_post_mix, w_in, b_in, attn_sinks, sg_ln_g, sg_ln_b,
           sg_w, sg_b, w_br_attn, w_br_sg, w_out, b_out, g_pre_ffn, g_post_ffn, w_router,
           b_router, w_mlp1, b_mlp1, w_mlp2, b_mlp2):
    assert x.shape == (BATCH, SEQ, D_MODEL) and w_mod.shape[0] == 1, "single-layer shapes only"
    row = lambda a: a[0].reshape(1, -1)
    bf = lambda a: a[0].astype(jnp.bfloat16)

    mod = _modulation(c, w_mod[0], b_mod[0])

    params = dict(
        sinks=attn_sinks[0], g_pre_mix=row(g_pre_mix), g_post_mix=row(g_post_mix),
        w_in=bf(w_in), b_in=row(b_in), sg_ln_g=row(sg_ln_g), sg_ln_b=row(sg_ln_b), sg_w=sg_w[0],
        sg_b=jnp.repeat(sg_b[0].T, SG_GROUP_DIM, axis=1),
        w_br_attn=bf(w_br_attn), w_br_sg=bf(w_br_sg), w_out=bf(w_out), b_out=row(b_out),
        g_pre_ffn=row(g_pre_ffn),
        w_router=jnp.pad(bf(w_router), ((0, 0), (0, LANES - N_EXPERTS))),
        b_router=jnp.pad(row(b_router), ((0, 0), (0, LANES - N_EXPERTS)), constant_values=_NEG_INF),
    )
    x1, h2_rows, route_i, route_g, counts = _mixer(x, mod, params)

    counts = counts[0, :N_EXPERTS].astype(jnp.int32)
    padded = (counts + MOE_BLOCK - 1) // MOE_BLOCK * MOE_BLOCK
    pends = jnp.cumsum(padded)
    pstarts = pends - padded
    block_start = jnp.arange(N_MOE_BLOCKS, dtype=jnp.int32) * MOE_BLOCK
    block_e = jnp.minimum(jnp.sum((pends[None, :] <= block_start[:, None]).astype(jnp.int32), axis=1),
                          N_EXPERTS - 1)
    expert_ids = jnp.arange(N_EXPERTS, dtype=jnp.int32)
    top_idx = route_i[:, :TOP_K]
    slot = (jnp.sum(jnp.where(top_idx[:, :, None] == expert_ids, pstarts, 0), axis=-1)
            + route_i[:, TOP_K:2 * TOP_K])
    free_start = jnp.concatenate([pstarts + counts, pends[-1:]])
    free_len = jnp.concatenate([padded - counts, N_SLOTS - pends[-1:]])
    free_end = jnp.cumsum(free_len)
    q = jnp.arange(N_PAD_SLOTS, dtype=jnp.int32)
    region = jnp.sum((free_end[None, :] <= q[:, None]).astype(jnp.int32), axis=1)
    region_hit = region[:, None] == jnp.arange(N_EXPERTS + 1, dtype=jnp.int32)
    pad_slots = q + jnp.sum(jnp.where(region_hit, free_start - (free_end - free_len), 0), axis=1)

    xs_rows = _dispatch(slot, pad_slots, h2_rows)
    n_used = (pends[-1:] // MOE_BLOCK).astype(jnp.int32)
    ys_rows = _experts(block_e, n_used, xs_rows, w_mlp1[0], b_mlp1[0], w_mlp2[0], b_mlp2[0])

    slot_tiles = slot.reshape(N_TOKENS // CMB_TILE, 1, TOP_K * CMB_TILE)
    out = _combine(slot_tiles, ys_rows, x1.reshape(N_TOKENS, D_MODEL), route_g, mod, row(g_post_ffn))
    return out.reshape(BATCH, SEQ, D_MODEL)
```

```python
import functools

import jax
import jax.numpy as jnp
import numpy as np
from jax import lax
from jax.experimental import pallas as pl
from jax.experimental.pallas import tpu as pltpu

D_MODEL = 1024
BATCH = 16
SEQ = 2048
N_TOKENS = BATCH * SEQ
HEAD_DIM = 64
N_Q_HEADS = 8
N_KV_HEADS = 2
Q_PER_KV = N_Q_HEADS // N_KV_HEADS
ATTN_WIDTH = N_Q_HEADS * HEAD_DIM
KV_WIDTH = N_KV_HEADS * HEAD_DIM
ATT_BLOCK = 128
N_SG_GROUPS = 8
SG_GROUP_DIM = 64
SG_WIDTH = N_SG_GROUPS * SG_GROUP_DIM
N_EXPERTS = 32
TOP_K = 4
D_EXPERT = D_MODEL
SWIGLU_LIMIT = 7.0
SWIGLU_ALPHA = 1.702
MOE_BLOCK = 512
NORM_EPS = 1e-6
N_ASSIGN = N_TOKENS * TOP_K
N_MOE_BLOCKS = N_ASSIGN // MOE_BLOCK + N_EXPERTS
N_SLOTS = N_MOE_BLOCKS * MOE_BLOCK

_Q0 = 0
_K0 = _Q0 + ATTN_WIDTH
_V0 = _K0 + KV_WIDTH
_U0 = _V0 + KV_WIDTH
_SV0 = _U0 + SG_WIDTH
_GA0 = _SV0 + SG_WIDTH
_GS0 = _GA0 + D_MODEL
IN_WIDTH = _GS0 + D_MODEL

LANES = 128
SUBLANES = 8
ROW_TILES = D_MODEL // LANES
assert ROW_TILES == SUBLANES
VMEM_LIMIT_BYTES = 56 * 1024 * 1024

MIX_TILE = 1024
CMB_TILE = 256
DSP_TILE = 256
DMA_UNROLL = 8
N_PAD_SLOTS = N_SLOTS - N_ASSIGN
DSP_PADS = N_PAD_SLOTS // (N_TOKENS // DSP_TILE)
assert DMA_UNROLL % TOP_K == 0 and DSP_PADS % DMA_UNROLL == 0

_NEG_INF = float("-inf")


def _rms(x, g):
    return x * lax.rsqrt(jnp.mean(x * x, axis=-1, keepdims=True) + NORM_EPS) * g


def _bf16(x):
    return x.astype(jnp.bfloat16)


def _sigmoid(x):
    return 0.5 * jnp.tanh(0.5 * x) + 0.5


def _dot(a, b):
    return jnp.dot(a, b, preferred_element_type=jnp.float32)


def _mod_kernel(c_ref, w_ref, b_ref, o_ref):
    c = c_ref[...]
    c_act = c * jax.nn.sigmoid(c)
    o_ref[...] = _dot(_bf16(c_act), _bf16(w_ref[...])) + b_ref[...]


def _modulation(c, w_mod, b_mod):
    n_out = w_mod.shape[1]
    tn = D_MODEL
    return pl.pallas_call(
        _mod_kernel,
        grid=(n_out // tn,),
        in_specs=[
            pl.BlockSpec((BATCH, D_MODEL), lambda n: (0, 0)),
            pl.BlockSpec((D_MODEL, tn), lambda n: (0, n)),
            pl.BlockSpec((1, tn), lambda n: (0, n)),
        ],
        out_specs=pl.BlockSpec((BATCH, tn), lambda n: (0, n)),
        out_shape=jax.ShapeDtypeStruct((BATCH, n_out), jnp.float32),
        compiler_params=pltpu.CompilerParams(dimension_semantics=("arbitrary",)),
        name="modulation",
    )(c, w_mod, b_mod.reshape(1, n_out))


def _attention_bias():
    slopes = 2.0 ** (-8.0 * np.arange(1, N_Q_HEADS + 1) / N_Q_HEADS)
    qi = np.arange(ATT_BLOCK)[:, None]
    kj = np.arange(2 * ATT_BLOCK)[None, :]
    dist = qi + ATT_BLOCK - kj
    in_win = (dist >= 0) & (dist < ATT_BLOCK)
    bias = np.where(in_win[None], -slopes[:, None, None] * dist[None].astype(np.float64), -np.inf)
    bias = bias.reshape(N_KV_HEADS, Q_PER_KV * ATT_BLOCK, 2 * ATT_BLOCK)
    return jnp.asarray(bias, dtype=jnp.float32)


def _mixer_kernel(sinks_ref, x_ref, mod_ref, gpre_ref, gpost_ref, win_ref, bin_ref, abias_ref,
                  lng_ref, lnb_ref, sgw_ref, sgb_ref, wba_ref, wbs_ref, wout_ref, bout_ref,
                  gffn_ref, wr_ref, br_ref,
                  x1_ref, h2_ref, ri_ref, rg_ref, cnt_ref,
                  kprev_ref, vprev_ref, carry_ref):
    b = pl.program_id(0)
    j = pl.program_id(1)
    tm = MIX_TILE
    n_blk = tm // ATT_BLOCK

    @pl.when((b == 0) & (j == 0))
    def _():
        carry_ref[...] = jnp.zeros_like(carry_ref)

    @pl.when(j == 0)
    def _():
        kprev_ref[...] = jnp.zeros_like(kprev_ref)
        vprev_ref[...] = jnp.zeros_like(vprev_ref)

    x = x_ref[0]
    mod = mod_ref[0]
    shift1, scale1, gate1, shift2, scale2, gate2 = (
        mod[:, i * D_MODEL:(i + 1) * D_MODEL] for i in range(6))

    h = _bf16(_rms(x, gpre_ref[...]) * (1.0 + scale1) + shift1)

    def proj(lo, width):
        return _dot(h, win_ref[:, lo:lo + width]) + bin_ref[:, lo:lo + width]

    q = proj(_Q0, ATTN_WIDTH)
    k = proj(_K0, KV_WIDTH)
    v = proj(_V0, KV_WIDTH)
    row_group = lax.broadcasted_iota(jnp.int32, (Q_PER_KV * ATT_BLOCK, 1), 0) // ATT_BLOCK
    col = lax.broadcasted_iota(jnp.int32, (Q_PER_KV * ATT_BLOCK, 2 * ATT_BLOCK), 1)
    y_att_blocks = []
    for i in range(n_blk):
        rows = slice(i * ATT_BLOCK, (i + 1) * ATT_BLOCK)
        if i == 0:
            k_prev, v_prev = kprev_ref[...], vprev_ref[...]
        else:
            prev = slice((i - 1) * ATT_BLOCK, i * ATT_BLOCK)
            k_prev, v_prev = k[prev], v[prev]
        k_cat = _bf16(jnp.concatenate([k_prev, k[rows]], axis=0))
        v_cat = _bf16(jnp.concatenate([v_prev, v[rows]], axis=0))
        q_blk = _bf16(q[rows])
        heads = [None] * N_Q_HEADS
        for hk in range(N_KV_HEADS):
            kv_cols = slice(hk * HEAD_DIM, (hk + 1) * HEAD_DIM)
            q_stack = jnp.concatenate(
                [q_blk[:, (hk * Q_PER_KV + g) * HEAD_DIM:(hk * Q_PER_KV + g + 1) * HEAD_DIM]
                 for g in range(Q_PER_KV)], axis=0)
            scores = lax.dot_general(q_stack, k_cat[:, kv_cols], (((1,), (1,)), ((), ())),
                                     preferred_element_type=jnp.float32)
            logits = scores * (HEAD_DIM ** -0.5) + abias_ref[hk]
            if i == 0:
                n_masked = jnp.where(j == 0, ATT_BLOCK, 0)
                logits = jnp.where(col < n_masked, _NEG_INF, logits)
            sink = jnp.zeros((Q_PER_KV * ATT_BLOCK, 1), jnp.float32)
            for g in range(Q_PER_KV):
                sink = jnp.where(row_group == g, sinks_ref[hk * Q_PER_KV + g], sink)
            m = jnp.maximum(jnp.max(logits, axis=-1, keepdims=True), sink)
            p = jnp.exp(logits - m)
            probs = p / (jnp.sum(p, axis=-1, keepdims=True) + jnp.exp(sink - m))
            out = _dot(_bf16(probs), v_cat[:, kv_cols])
            for g in range(Q_PER_KV):
                heads[hk * Q_PER_KV + g] = out[g * ATT_BLOCK:(g + 1) * ATT_BLOCK]
        y_att_blocks.append(jnp.concatenate(heads, axis=1))
    kprev_ref[...] = k[tm - ATT_BLOCK:]
    vprev_ref[...] = v[tm - ATT_BLOCK:]
    y_att = jnp.concatenate(y_att_blocks, axis=0)
    branch_att = _dot(_bf16(y_att), wba_ref[...])

    u = jax.nn.gelu(proj(_U0, SG_WIDTH))
    sv = jax.nn.gelu(proj(_SV0, SG_WIDTH))
    mu = jnp.mean(sv, axis=-1, keepdims=True)
    var = jnp.mean(jnp.square(sv - mu), axis=-1, keepdims=True)
    vn = _bf16((sv - mu) * lax.rsqrt(var + NORM_EPS) * lng_ref[...] + lnb_ref[...])
    causal = (lax.broadcasted_iota(jnp.int32, (ATT_BLOCK, ATT_BLOCK), 0)
              >= lax.broadcasted_iota(jnp.int32, (ATT_BLOCK, ATT_BLOCK), 1))
    w_sp = [_bf16(jnp.where(causal, sgw_ref[g], 0.0)) for g in range(N_SG_GROUPS)]
    mixed_chunks = []
    for cidx in range(n_blk):
        rows = slice(cidx * ATT_BLOCK, (cidx + 1) * ATT_BLOCK)
        groups = [_dot(w_sp[g], vn[rows, g * SG_GROUP_DIM:(g + 1) * SG_GROUP_DIM])
                  for g in range(N_SG_GROUPS)]
        mixed_chunks.append(jnp.concatenate(groups, axis=1) + sgb_ref[...])
    y_sg = u * jnp.concatenate(mixed_chunks, axis=0)
    branch_sg = _dot(_bf16(y_sg), wbs_ref[...])

    merged = (_sigmoid(proj(_GA0, D_MODEL)) * branch_att
              + _sigmoid(proj(_GS0, D_MODEL)) * branch_sg)
    y = _dot(_bf16(merged), wout_ref[...]) + bout_ref[...]
    x1 = x + gate1 * _rms(y, gpost_ref[...])
    x1_ref[0] = x1

    h2 = _rms(x1, gffn_ref[...]) * (1.0 + scale2) + shift2
    for s in range(ROW_TILES):
        h2_ref[pl.ds(s, tm, stride=ROW_TILES), :] = h2[:, s * LANES:(s + 1) * LANES]
    logits = _dot(_bf16(h2), wr_ref[...]) + br_ref[...]
    lane = lax.broadcasted_iota(jnp.int32, (tm, LANES), 1).astype(jnp.float32)
    top_val, top_idx = [], []
    selected = jnp.zeros((tm, LANES), jnp.float32)
    for _ in range(TOP_K):
        mval = jnp.max(logits, axis=-1, keepdims=True)
        midx = jnp.min(jnp.where(logits == mval, lane, float(LANES)), axis=-1, keepdims=True)
        hit = lane == midx
        selected = jnp.where(hit, 1.0, selected)
        logits = jnp.where(hit, _NEG_INF, logits)
        top_val.append(mval)
        top_idx.append(midx)
    expv = [jnp.exp(tv - top_val[0]) for tv in top_val]
    denom = expv[0] + expv[1] + expv[2] + expv[3]
    strict_lower = (lax.broadcasted_iota(jnp.int32, (tm, tm), 0)
                    > lax.broadcasted_iota(jnp.int32, (tm, tm), 1))
    before = _dot(_bf16(jnp.where(strict_lower, 1.0, 0.0)), _bf16(selected)) + carry_ref[0:1, :]
    route_i = jnp.zeros((tm, LANES), jnp.float32)
    route_g = jnp.zeros((tm, LANES), jnp.float32)
    for kk in range(TOP_K):
        rank = jnp.sum(jnp.where(lane == top_idx[kk], before, 0.0), axis=-1, keepdims=True)
        route_i = jnp.where(lane == float(kk), top_idx[kk], route_i)
        route_i = jnp.where(lane == float(TOP_K + kk), rank, route_i)
        route_g = jnp.where(lane == float(kk), expv[kk] / denom, route_g)
    ri_ref[...] = route_i.astype(jnp.int32)
    rg_ref[...] = route_g
    total = carry_ref[0:1, :] + jnp.sum(selected, axis=0, keepdims=True)
    carry_ref[...] = jnp.broadcast_to(total, carry_ref.shape)
    cnt_ref[...] = jnp.broadcast_to(total, cnt_ref.shape)


def _mixer(x, mod, p):
    tm = MIX_TILE
    n_j = SEQ // tm
    const2 = lambda shape: pl.BlockSpec(shape, lambda b, j, s: (0, 0), pipeline_mode=pl.Buffered(1))
    const3 = lambda shape: pl.BlockSpec(shape, lambda b, j, s: (0, 0, 0), pipeline_mode=pl.Buffered(1))
    tile_rows = lambda b, j, s: (b * n_j + j, 0)
    in_specs = [
        pl.BlockSpec((1, tm, D_MODEL), lambda b, j, s: (b, j, 0)),
        pl.BlockSpec((1, 1, 6 * D_MODEL), lambda b, j, s: (b, 0, 0)),
        const2((1, D_MODEL)), const2((1, D_MODEL)),
        const2((D_MODEL, IN_WIDTH)), const2((1, IN_WIDTH)),
        const3((N_KV_HEADS, Q_PER_KV * ATT_BLOCK, 2 * ATT_BLOCK)),
        const2((1, SG_WIDTH)), const2((1, SG_WIDTH)),
        const3((N_SG_GROUPS, ATT_BLOCK, ATT_BLOCK)), const2((ATT_BLOCK, SG_WIDTH)),
        const2((ATTN_WIDTH, D_MODEL)), const2((SG_WIDTH, D_MODEL)),
        const2((D_MODEL, D_MODEL)), const2((1, D_MODEL)),
        const2((1, D_MODEL)),
        const2((D_MODEL, LANES)), const2((1, LANES)),
    ]
    out_specs = [
        pl.BlockSpec((1, tm, D_MODEL), lambda b, j, s: (b, j, 0)),
        pl.BlockSpec((tm * ROW_TILES, LANES), tile_rows),
        pl.BlockSpec((tm, LANES), tile_rows),
        pl.BlockSpec((tm, LANES), tile_rows),
        pl.BlockSpec((SUBLANES, LANES), lambda b, j, s: (0, 0)),
    ]
    out_shape = [
        jax.ShapeDtypeStruct((BATCH, SEQ, D_MODEL), jnp.float32),
        jax.ShapeDtypeStruct((N_TOKENS * ROW_TILES, LANES), jnp.float32),
        jax.ShapeDtypeStruct((N_TOKENS, LANES), jnp.int32),
        jax.ShapeDtypeStruct((N_TOKENS, LANES), jnp.float32),
        jax.ShapeDtypeStruct((SUBLANES, LANES), jnp.float32),
    ]
    return pl.pallas_call(
        _mixer_kernel,
        grid_spec=pltpu.PrefetchScalarGridSpec(
            num_scalar_prefetch=1,
            grid=(BATCH, n_j),
            in_specs=in_specs,
            out_specs=out_specs,
            scratch_shapes=[
                pltpu.VMEM((ATT_BLOCK, KV_WIDTH), jnp.float32),
                pltpu.VMEM((ATT_BLOCK, KV_WIDTH), jnp.float32),
                pltpu.VMEM((SUBLANES, LANES), jnp.float32),
            ]),
        out_shape=out_shape,
        compiler_params=pltpu.CompilerParams(
            dimension_semantics=("arbitrary", "arbitrary"), vmem_limit_bytes=VMEM_LIMIT_BYTES),
        name="mixer_router",
    )(p["sinks"], x, mod.reshape(BATCH, 1, 6 * D_MODEL), p["g_pre_mix"], p["g_post_mix"],
      p["w_in"], p["b_in"], _attention_bias(), p["sg_ln_g"], p["sg_ln_b"], p["sg_w"], p["sg_b"],
      p["w_br_attn"], p["w_br_sg"], p["w_out"], p["b_out"], p["g_pre_ffn"],
      p["w_router"], p["b_router"])


def _row_copy(src, src_row, dst, dst_row, sem):
    return pltpu.make_async_copy(
        src.at[pl.ds(pl.multiple_of(src_row * ROW_TILES, ROW_TILES), ROW_TILES)],
        dst.at[pl.ds(pl.multiple_of(dst_row * ROW_TILES, ROW_TILES), ROW_TILES)], sem)


def _wait_rows(src, dst, n_rows, sem):
    pltpu.make_async_copy(src.at[pl.ds(0, n_rows * ROW_TILES)],
                          dst.at[pl.ds(0, n_rows * ROW_TILES)], sem).wait()


def _rows_to_matrix(rows_ref, first_row, n_rows):
    return jnp.concatenate(
        [rows_ref[pl.ds(first_row * ROW_TILES + s, n_rows, stride=ROW_TILES), :]
         for s in range(ROW_TILES)], axis=1)


def _dispatch_kernel(slot_ref, pad_ref, h2_ref, xs_hbm, sem):
    def issue(g, carry):
        for r in range(DMA_UNROLL):
            a = g * DMA_UNROLL + r
            tok = g * (DMA_UNROLL // TOP_K) + r // TOP_K
            _row_copy(h2_ref, tok, xs_hbm, slot_ref[0, 0, a], sem).start(priority=r % 2)
        return carry

    lax.fori_loop(0, DSP_TILE * TOP_K // DMA_UNROLL, issue, 0)

    def issue_pad(g, carry):
        for r in range(DMA_UNROLL):
            _row_copy(h2_ref, 0, xs_hbm, pad_ref[0, 0, g * DMA_UNROLL + r], sem).start(priority=r % 2)
        return carry

    lax.fori_loop(0, DSP_PADS // DMA_UNROLL, issue_pad, 0)
    n_rows = (DSP_TILE * TOP_K + DSP_PADS) * ROW_TILES
    pltpu.make_async_copy(xs_hbm.at[pl.ds(0, n_rows)], xs_hbm.at[pl.ds(n_rows, n_rows)], sem).wait()


def _dispatch(slot, pad_slots, h2_rows):
    n_steps = N_TOKENS // DSP_TILE
    return pl.pallas_call(
        _dispatch_kernel,
        grid=(n_steps,),
        in_specs=[
            pl.BlockSpec((1, 1, DSP_TILE * TOP_K), lambda i: (i, 0, 0), memory_space=pltpu.SMEM),
            pl.BlockSpec((1, 1, DSP_PADS), lambda i: (i, 0, 0), memory_space=pltpu.SMEM),
            pl.BlockSpec((DSP_TILE * ROW_TILES, LANES), lambda i: (i, 0)),
        ],
        out_specs=pl.BlockSpec(memory_space=pl.ANY),
        out_shape=jax.ShapeDtypeStruct((N_SLOTS * ROW_TILES, LANES), jnp.float32),
        scratch_shapes=[pltpu.SemaphoreType.DMA],
        compiler_params=pltpu.CompilerParams(dimension_semantics=("arbitrary",)),
        name="dispatch",
    )(slot.reshape(n_steps, 1, DSP_TILE * TOP_K), pad_slots.reshape(n_steps, 1, DSP_PADS), h2_rows)


def _expert_kernel(be_ref, nused_ref, xs_ref, w1_ref, b1_ref, w2_ref, b2_ref, ys_ref, w1b_ref, w2b_ref):
    i = pl.program_id(0)
    used = i < nused_ref[0]

    @pl.when(used & ((i == 0) | (be_ref[i] != be_ref[jnp.maximum(i - 1, 0)])))
    def _():
        w1b_ref[...] = _bf16(w1_ref[0])
        w2b_ref[...] = _bf16(w2_ref[0])

    @pl.when(used)
    def _():
        xb = _bf16(_rows_to_matrix(xs_ref, 0, MOE_BLOCK))
        hb = _dot(xb, w1b_ref[...]) + b1_ref[0]
        g = jnp.minimum(hb[:, :D_EXPERT], SWIGLU_LIMIT)
        lin = jnp.clip(hb[:, D_EXPERT:], -SWIGLU_LIMIT, SWIGLU_LIMIT)
        act = g * _sigmoid(SWIGLU_ALPHA * g) * (lin + 1.0)
        y = _dot(_bf16(act), w2b_ref[...]) + b2_ref[0]
        for s in range(ROW_TILES):
            ys_ref[pl.ds(s, MOE_BLOCK, stride=ROW_TILES), :] = y[:, s * LANES:(s + 1) * LANES]

    @pl.when(jnp.logical_not(used))
    def _():
        ys_ref[...] = jnp.zeros_like(ys_ref)


def _experts(block_e, n_used, xs_rows, w1, b1, w2, b2):
    return pl.pallas_call(
        _expert_kernel,
        grid_spec=pltpu.PrefetchScalarGridSpec(
            num_scalar_prefetch=2,
            grid=(N_MOE_BLOCKS,),
            in_specs=[
                pl.BlockSpec((MOE_BLOCK * ROW_TILES, LANES), lambda i, be, nu: (i, 0)),
                pl.BlockSpec((1, D_MODEL, 2 * D_EXPERT), lambda i, be, nu: (be[i], 0, 0)),
                pl.BlockSpec((1, 1, 2 * D_EXPERT), lambda i, be, nu: (be[i], 0, 0)),
                pl.BlockSpec((1, D_EXPERT, D_MODEL), lambda i, be, nu: (be[i], 0, 0)),
                pl.BlockSpec((1, 1, D_MODEL), lambda i, be, nu: (be[i], 0, 0)),
            ],
            out_specs=pl.BlockSpec((MOE_BLOCK * ROW_TILES, LANES), lambda i, be, nu: (i, 0)),
            scratch_shapes=[
                pltpu.VMEM((D_MODEL, 2 * D_EXPERT), jnp.bfloat16),
                pltpu.VMEM((D_EXPERT, D_MODEL), jnp.bfloat16),
            ],
        ),
        out_shape=jax.ShapeDtypeStruct((N_SLOTS * ROW_TILES, LANES), jnp.float32),
        compiler_params=pltpu.CompilerParams(
            dimension_semantics=("arbitrary",), vmem_limit_bytes=VMEM_LIMIT_BYTES),
        name="experts",
    )(block_e, n_used, xs_rows,
      w1, b1.reshape(N_EXPERTS, 1, 2 * D_EXPERT), w2, b2.reshape(N_EXPERTS, 1, D_MODEL))


def _combine_kernel(slot0_ref, slot_next_ref, ys_hbm, x1_ref, rg_ref, mod_ref, gpost_ref, o_ref,
                    buf_ref, sems):
    tm = CMB_TILE
    n_rows = TOP_K * tm
    i = pl.program_id(0)
    n_steps = pl.num_programs(0)

    def gather(slot_ref, half):
        def issue(g, carry):
            for r in range(DMA_UNROLL):
                tok = g * (DMA_UNROLL // TOP_K) + r // TOP_K
                _row_copy(ys_hbm, slot_ref[0, 0, g * DMA_UNROLL + r], buf_ref.at[half],
                          (r % TOP_K) * tm + tok, sems.at[half]).start(priority=r % 2)
            return carry
        lax.fori_loop(0, n_rows // DMA_UNROLL, issue, 0)

    @pl.when(i == 0)
    def _():
        gather(slot0_ref, 0)

    @pl.when(i + 1 < n_steps)
    def _():
        gather(slot_next_ref, (i + 1) % 2)

    cur = i % 2
    _wait_rows(ys_hbm, buf_ref.at[cur], n_rows, sems.at[cur])
    gates = rg_ref[...]
    rows = buf_ref.at[cur]
    y = jnp.zeros((tm, D_MODEL), jnp.float32)
    for kk in range(TOP_K):
        y = y + _rows_to_matrix(rows, kk * tm, tm) * gates[:, kk:kk + 1]
    gate2 = mod_ref[0][:, 5 * D_MODEL:6 * D_MODEL]
    o_ref[...] = x1_ref[...] + gate2 * _rms(y, gpost_ref[...])


def _combine(slot_tiles, ys_rows, x1, route_g, mod, g_post_ffn):
    tm = CMB_TILE
    n_tiles = N_TOKENS // tm
    tiles_per_seq = SEQ // tm
    return pl.pallas_call(
        _combine_kernel,
        grid=(n_tiles,),
        in_specs=[
            pl.BlockSpec((1, 1, TOP_K * tm), lambda i: (0, 0, 0), memory_space=pltpu.SMEM),
            pl.BlockSpec((1, 1, TOP_K * tm), lambda i: (jnp.minimum(i + 1, n_tiles - 1), 0, 0),
                         memory_space=pltpu.SMEM),
            pl.BlockSpec(memory_space=pl.ANY),
            pl.BlockSpec((tm, D_MODEL), lambda i: (i, 0)),
            pl.BlockSpec((tm, LANES), lambda i: (i, 0)),
            pl.BlockSpec((1, 1, 6 * D_MODEL), lambda i: (i // tiles_per_seq, 0, 0)),
            pl.BlockSpec((1, D_MODEL), lambda i: (0, 0)),
        ],
        out_specs=pl.BlockSpec((tm, D_MODEL), lambda i: (i, 0)),
        out_shape=jax.ShapeDtypeStruct((N_TOKENS, D_MODEL), jnp.float32),
        scratch_shapes=[
            pltpu.VMEM((2, TOP_K * tm * ROW_TILES, LANES), jnp.float32),
            pltpu.SemaphoreType.DMA((2,)),
        ],
        compiler_params=pltpu.CompilerParams(
            dimension_semantics=("arbitrary",), vmem_limit_bytes=VMEM_LIMIT_BYTES),
        name="combine",
    )(slot_tiles, slot_tiles, ys_rows, x1, route_g, mod.reshape(BATCH, 1, 6 * D_MODEL),
      g_post_ffn)


def kernel(x, c, w_mod, b_mod, g_pre_mix, g_post_mix, w_in, b_in, attn_sinks, sg_ln_g, sg_ln_b,
           sg_w, sg_b, w_br_attn, w_br_sg, w_out, b_out, g_pre_ffn, g_post_ffn, w_router,
           b_router, w_mlp1, b_mlp1, w_mlp2, b_mlp2):
    assert x.shape == (BATCH, SEQ, D_MODEL) and w_mod.shape[0] == 1, "single-layer shapes only"
    row = lambda a: a[0].reshape(1, -1)
    bf = lambda a: a[0].astype(jnp.bfloat16)

    mod = _modulation(c, w_mod[0], b_mod[0])

    params = dict(
        sinks=attn_sinks[0], g_pre_mix=row(g_pre_mix), g_post_mix=row(g_post_mix),
        w_in=bf(w_in), b_in=row(b_in), sg_ln_g=row(sg_ln_g), sg_ln_b=row(sg_ln_b), sg_w=sg_w[0],
        sg_b=jnp.repeat(sg_b[0].T, SG_GROUP_DIM, axis=1),
        w_br_attn=bf(w_br_attn), w_br_sg=bf(w_br_sg), w_out=bf(w_out), b_out=row(b_out),
        g_pre_ffn=row(g_pre_ffn),
        w_router=jnp.pad(bf(w_router), ((0, 0), (0, LANES - N_EXPERTS))),
        b_router=jnp.pad(row(b_router), ((0, 0), (0, LANES - N_EXPERTS)), constant_values=_NEG_INF),
    )
    x1, h2_rows, route_i, route_g, counts = _mixer(x, mod, params)

    counts = counts[0, :N_EXPERTS].astype(jnp.int32)
    padded = (counts + MOE_BLOCK - 1) // MOE_BLOCK * MOE_BLOCK
    pends = jnp.cumsum(padded)
    pstarts = pends - padded
    block_start = jnp.arange(N_MOE_BLOCKS, dtype=jnp.int32) * MOE_BLOCK
    block_e = jnp.minimum(jnp.sum((pends[None, :] <= block_start[:, None]).astype(jnp.int32), axis=1),
                          N_EXPERTS - 1)
    expert_ids = jnp.arange(N_EXPERTS, dtype=jnp.int32)
    top_idx = route_i[:, :TOP_K]
    slot = (jnp.sum(jnp.where(top_idx[:, :, None] == expert_ids, pstarts, 0), axis=-1)
            + route_i[:, TOP_K:2 * TOP_K])
    free_start = jnp.concatenate([pstarts + counts, pends[-1:]])
    free_len = jnp.concatenate([padded - counts, N_SLOTS - pends[-1:]])
    free_end = jnp.cumsum(free_len)
    q = jnp.arange(N_PAD_SLOTS, dtype=jnp.int32)
    region = jnp.sum((free_end[None, :] <= q[:, None]).astype(jnp.int32), axis=1)
    region_hit = region[:, None] == jnp.arange(N_EXPERTS + 1, dtype=jnp.int32)
    pad_slots = q + jnp.sum(jnp.where(region_hit, free_start - (free_end - free_len), 0), axis=1)

    xs_rows = _dispatch(slot, pad_slots, h2_rows)
    n_used = (pends[-1:] // MOE_BLOCK).astype(jnp.int32)
    ys_rows = _experts(block_e, n_used, xs_rows, w_mlp1[0], b_mlp1[0], w_mlp2[0], b_mlp2[0])

    slot_tiles = slot.reshape(N_TOKENS // CMB_TILE, 1, TOP_K * CMB_TILE)
    out = _combine(slot_tiles, ys_rows, x1.reshape(N_TOKENS, D_MODEL), route_g, mod, row(g_post_ffn))
    return out.reshape(BATCH, SEQ, D_MODEL)
```

```python
import jax
import jax.numpy as jnp
import numpy as np
from jax import lax
from jax.experimental import pallas as pl
from jax.experimental.pallas import tpu as pltpu

D_MODEL = 1024
BATCH = 16
SEQ = 2048
N_TOKENS = BATCH * SEQ
HEAD_DIM = 64
N_Q_HEADS = 8
N_KV_HEADS = 2
Q_PER_KV = N_Q_HEADS // N_KV_HEADS
ATTN_WIDTH = N_Q_HEADS * HEAD_DIM
KV_WIDTH = N_KV_HEADS * HEAD_DIM
ATT_BLOCK = 128
N_SG_GROUPS = 8
SG_GROUP_DIM = 64
SG_WIDTH = N_SG_GROUPS * SG_GROUP_DIM
N_EXPERTS = 32
TOP_K = 4
D_EXPERT = D_MODEL
SWIGLU_LIMIT = 7.0
SWIGLU_ALPHA = 1.702
MOE_BLOCK = 512
NORM_EPS = 1e-6
N_ASSIGN = N_TOKENS * TOP_K
N_MOE_BLOCKS = N_ASSIGN // MOE_BLOCK + N_EXPERTS
N_SLOTS = N_MOE_BLOCKS * MOE_BLOCK

_Q0 = 0
_K0 = _Q0 + ATTN_WIDTH
_V0 = _K0 + KV_WIDTH
_U0 = _V0 + KV_WIDTH
_SV0 = _U0 + SG_WIDTH
_GA0 = _SV0 + SG_WIDTH
_GS0 = _GA0 + D_MODEL
IN_WIDTH = _GS0 + D_MODEL

LANES = 128
SUBLANES = 8
ROW_TILES = D_MODEL // LANES
assert ROW_TILES == SUBLANES
VMEM_LIMIT_BYTES = 56 * 1024 * 1024

MIX_TILE = 1024
CMB_TILE = 512
DSP_TILE = 256
DMA_UNROLL = 8
N_PAD_SLOTS = N_SLOTS - N_ASSIGN
DSP_PADS = N_PAD_SLOTS // (N_TOKENS // DSP_TILE)
assert DMA_UNROLL % TOP_K == 0 and DSP_PADS % DMA_UNROLL == 0

_NEG_INF = float("-inf")


def _rms(x, g):
    return x * lax.rsqrt(jnp.mean(x * x, axis=-1, keepdims=True) + NORM_EPS) * g


def _bf16(x):
    return x.astype(jnp.bfloat16)


def _sigmoid(x):
    return 0.5 * jnp.tanh(0.5 * x) + 0.5


def _dot(a, b):
    return jnp.dot(a, b, preferred_element_type=jnp.float32)


def _mod_kernel(c_ref, w_ref, b_ref, o_ref):
    c = c_ref[...]
    c_act = c * jax.nn.sigmoid(c)
    o_ref[...] = _dot(_bf16(c_act), _bf16(w_ref[...])) + b_ref[...]


def _modulation(c, w_mod, b_mod):
    n_out = w_mod.shape[1]
    tn = D_MODEL
    return pl.pallas_call(
        _mod_kernel,
        grid=(n_out // tn,),
        in_specs=[
            pl.BlockSpec((BATCH, D_MODEL), lambda n: (0, 0)),
            pl.BlockSpec((D_MODEL, tn), lambda n: (0, n)),
            pl.BlockSpec((1, tn), lambda n: (0, n)),
        ],
        out_specs=pl.BlockSpec((BATCH, tn), lambda n: (0, n)),
        out_shape=jax.ShapeDtypeStruct((BATCH, n_out), jnp.float32),
        compiler_params=pltpu.CompilerParams(dimension_semantics=("arbitrary",)),
        name="modulation",
    )(c, w_mod, b_mod.reshape(1, n_out))


def _attention_bias():
    slopes = 2.0 ** (-8.0 * np.arange(1, N_Q_HEADS + 1) / N_Q_HEADS)
    qi = np.arange(ATT_BLOCK)[:, None]
    kj = np.arange(2 * ATT_BLOCK)[None, :]
    dist = qi + ATT_BLOCK - kj
    in_win = (dist >= 0) & (dist < ATT_BLOCK)
    bias = np.where(in_win[None], -slopes[:, None, None] * dist[None].astype(np.float64), -np.inf)
    bias = bias.reshape(N_KV_HEADS, Q_PER_KV * ATT_BLOCK, 2 * ATT_BLOCK)
    return jnp.asarray(bias, dtype=jnp.float32)


def _mixer_kernel(sinks_ref, x_ref, mod_ref, gpre_ref, gpost_ref, win_ref, bin_ref, abias_ref,
                  lng_ref, lnb_ref, sgw_ref, sgb_ref, wba_ref, wbs_ref, wout_ref, bout_ref,
                  gffn_ref, wr_ref, br_ref,
                  x1_ref, h2_ref, ri_ref, rg_ref, cnt_ref,
                  kprev_ref, vprev_ref, carry_ref):
    b = pl.program_id(0)
    j = pl.program_id(1)
    tm = MIX_TILE
    n_blk = tm // ATT_BLOCK

    @pl.when((b == 0) & (j == 0))
    def _():
        carry_ref[...] = jnp.zeros_like(carry_ref)

    @pl.when(j == 0)
    def _():
        kprev_ref[...] = jnp.zeros_like(kprev_ref)
        vprev_ref[...] = jnp.zeros_like(vprev_ref)

    x = x_ref[0]
    mod = mod_ref[0]
    shift1, scale1, gate1, shift2, scale2, gate2 = (
        mod[:, i * D_MODEL:(i + 1) * D_MODEL] for i in range(6))

    h = _bf16(_rms(x, gpre_ref[...]) * (1.0 + scale1) + shift1)

    def proj(lo, width):
        return _dot(h, win_ref[:, lo:lo + width]) + bin_ref[:, lo:lo + width]

    q = proj(_Q0, ATTN_WIDTH)
    k = proj(_K0, KV_WIDTH)
    v = proj(_V0, KV_WIDTH)
    row_group = lax.broadcasted_iota(jnp.int32, (Q_PER_KV * ATT_BLOCK, 1), 0) // ATT_BLOCK
    col = lax.broadcasted_iota(jnp.int32, (Q_PER_KV * ATT_BLOCK, 2 * ATT_BLOCK), 1)
    y_att_blocks = []
    for i in range(n_blk):
        rows = slice(i * ATT_BLOCK, (i + 1) * ATT_BLOCK)
        if i == 0:
            k_prev, v_prev = kprev_ref[...], vprev_ref[...]
        else:
            prev = slice((i - 1) * ATT_BLOCK, i * ATT_BLOCK)
            k_prev, v_prev = k[prev], v[prev]
        k_cat = _bf16(jnp.concatenate([k_prev, k[rows]], axis=0))
        v_cat = _bf16(jnp.concatenate([v_prev, v[rows]], axis=0))
        q_blk = _bf16(q[rows])
        heads = [None] * N_Q_HEADS
        for hk in range(N_KV_HEADS):
            kv_cols = slice(hk * HEAD_DIM, (hk + 1) * HEAD_DIM)
            q_stack = jnp.concatenate(
                [q_blk[:, (hk * Q_PER_KV + g) * HEAD_DIM:(hk * Q_PER_KV + g + 1) * HEAD_DIM]
                 for g in range(Q_PER_KV)], axis=0)
            scores = lax.dot_general(q_stack, k_cat[:, kv_cols], (((1,), (1,)), ((), ())),
                                     preferred_element_type=jnp.float32)
            logits = scores * (HEAD_DIM ** -0.5) + abias_ref[hk]
            if i == 0:
                n_masked = jnp.where(j == 0, ATT_BLOCK, 0)
                logits = jnp.where(col < n_masked, _NEG_INF, logits)
            sink = jnp.zeros((Q_PER_KV * ATT_BLOCK, 1), jnp.float32)
            for g in range(Q_PER_KV):
                sink = jnp.where(row_group == g, sinks_ref[hk * Q_PER_KV + g], sink)
            m = jnp.maximum(jnp.max(logits, axis=-1, keepdims=True), sink)
            p = jnp.exp(logits - m)
            probs = p / (jnp.sum(p, axis=-1, keepdims=True) + jnp.exp(sink - m))
            out = _dot(_bf16(probs), v_cat[:, kv_cols])
            for g in range(Q_PER_KV):
                heads[hk * Q_PER_KV + g] = out[g * ATT_BLOCK:(g + 1) * ATT_BLOCK]
        y_att_blocks.append(jnp.concatenate(heads, axis=1))
    kprev_ref[...] = k[tm - ATT_BLOCK:]
    vprev_ref[...] = v[tm - ATT_BLOCK:]
    y_att = jnp.concatenate(y_att_blocks, axis=0)
    branch_att = _dot(_bf16(y_att), wba_ref[...])

    u = jax.nn.gelu(proj(_U0, SG_WIDTH))
    sv = jax.nn.gelu(proj(_SV0, SG_WIDTH))
    mu = jnp.mean(sv, axis=-1, keepdims=True)
    var = jnp.mean(jnp.square(sv - mu), axis=-1, keepdims=True)
    vn = _bf16((sv - mu) * lax.rsqrt(var + NORM_EPS) * lng_ref[...] + lnb_ref[...])
    causal = (lax.broadcasted_iota(jnp.int32, (ATT_BLOCK, ATT_BLOCK), 0)
              >= lax.broadcasted_iota(jnp.int32, (ATT_BLOCK, ATT_BLOCK), 1))
    w_sp = [_bf16(jnp.where(causal, sgw_ref[g], 0.0)) for g in range(N_SG_GROUPS)]
    mixed_chunks = []
    for cidx in range(n_blk):
        rows = slice(cidx * ATT_BLOCK, (cidx + 1) * ATT_BLOCK)
        groups = [_dot(w_sp[g], vn[rows, g * SG_GROUP_DIM:(g + 1) * SG_GROUP_DIM])
                  for g in range(N_SG_GROUPS)]
        mixed_chunks.append(jnp.concatenate(groups, axis=1) + sgb_ref[...])
    y_sg = u * jnp.concatenate(mixed_chunks, axis=0)
    branch_sg = _dot(_bf16(y_sg), wbs_ref[...])

    merged = (_sigmoid(proj(_GA0, D_MODEL)) * branch_att
              + _sigmoid(proj(_GS0, D_MODEL)) * branch_sg)
    y = _dot(_bf16(merged), wout_ref[...]) + bout_ref[...]
    x1 = x + gate1 * _rms(y, gpost_ref[...])
    x1_ref[0] = x1

    h2 = _rms(x1, gffn_ref[...]) * (1.0 + scale2) + shift2
    for s in range(ROW_TILES):
        h2_ref[pl.ds(s, tm, stride=ROW_TILES), :] = h2[:, s * LANES:(s + 1) * LANES]
    logits = _dot(_bf16(h2), wr_ref[...]) + br_ref[...]
    lane = lax.broadcasted_iota(jnp.int32, (tm, LANES), 1).astype(jnp.float32)
    top_val, top_idx = [], []
    selected = jnp.zeros((tm, LANES), jnp.float32)
    for _ in range(TOP_K):
        mval = jnp.max(logits, axis=-1, keepdims=True)
        midx = jnp.min(jnp.where(logits == mval, lane, float(LANES)), axis=-1, keepdims=True)
        hit = lane == midx
        selected = jnp.where(hit, 1.0, selected)
        logits = jnp.where(hit, _NEG_INF, logits)
        top_val.append(mval)
        top_idx.append(midx)
    expv = [jnp.exp(tv - top_val[0]) for tv in top_val]
    denom = expv[0] + expv[1] + expv[2] + expv[3]
    strict_lower = _bf16(jnp.where(
        lax.broadcasted_iota(jnp.int32, (ATT_BLOCK, ATT_BLOCK), 0)
        > lax.broadcasted_iota(jnp.int32, (ATT_BLOCK, ATT_BLOCK), 1), 1.0, 0.0))
    total = carry_ref[0:1, :]
    before_blocks = []
    for i in range(n_blk):
        sel_blk = selected[i * ATT_BLOCK:(i + 1) * ATT_BLOCK]
        before_blocks.append(_dot(strict_lower, _bf16(sel_blk)) + total)
        total = total + jnp.sum(sel_blk, axis=0, keepdims=True)
    before = jnp.concatenate(before_blocks, axis=0)
    route_i = jnp.zeros((tm, LANES), jnp.float32)
    route_g = jnp.zeros((tm, LANES), jnp.float32)
    for kk in range(TOP_K):
        rank = jnp.sum(jnp.where(lane == top_idx[kk], before, 0.0), axis=-1, keepdims=True)
        route_i = jnp.where(lane == float(kk), top_idx[kk], route_i)
        route_i = jnp.where(lane == float(TOP_K + kk), rank, route_i)
        route_g = jnp.where(lane == float(kk), expv[kk] / denom, route_g)
    ri_ref[...] = route_i.astype(jnp.int32)
    rg_ref[...] = route_g
    carry_ref[...] = jnp.broadcast_to(total, carry_ref.shape)
    cnt_ref[...] = jnp.broadcast_to(total, cnt_ref.shape)


def _mixer(x, mod, p):
    tm = MIX_TILE
    n_j = SEQ // tm
    const2 = lambda shape: pl.BlockSpec(shape, lambda b, j, s: (0, 0), pipeline_mode=pl.Buffered(1))
    const3 = lambda shape: pl.BlockSpec(shape, lambda b, j, s: (0, 0, 0), pipeline_mode=pl.Buffered(1))
    tile_rows = lambda b, j, s: (b * n_j + j, 0)
    in_specs = [
        pl.BlockSpec((1, tm, D_MODEL), lambda b, j, s: (b, j, 0)),
        pl.BlockSpec((1, 1, 6 * D_MODEL), lambda b, j, s: (b, 0, 0)),
        const2((1, D_MODEL)), const2((1, D_MODEL)),
        const2((D_MODEL, IN_WIDTH)), const2((1, IN_WIDTH)),
        const3((N_KV_HEADS, Q_PER_KV * ATT_BLOCK, 2 * ATT_BLOCK)),
        const2((1, SG_WIDTH)), const2((1, SG_WIDTH)),
        const3((N_SG_GROUPS, ATT_BLOCK, ATT_BLOCK)), const2((ATT_BLOCK, SG_WIDTH)),
        const2((ATTN_WIDTH, D_MODEL)), const2((SG_WIDTH, D_MODEL)),
        const2((D_MODEL, D_MODEL)), const2((1, D_MODEL)),
        const2((1, D_MODEL)),
        const2((D_MODEL, LANES)), const2((1, LANES)),
    ]
    out_specs = [
        pl.BlockSpec((1, tm, D_MODEL), lambda b, j, s: (b, j, 0)),
        pl.BlockSpec((tm * ROW_TILES, LANES), tile_rows),
        pl.BlockSpec((tm, LANES), tile_rows),
        pl.BlockSpec((tm, LANES), tile_rows),
        pl.BlockSpec((SUBLANES, LANES), lambda b, j, s: (0, 0)),
    ]
    out_shape = [
        jax.ShapeDtypeStruct((BATCH, SEQ, D_MODEL), jnp.float32),
        jax.ShapeDtypeStruct((N_TOKENS * ROW_TILES, LANES), jnp.float32),
        jax.ShapeDtypeStruct((N_TOKENS, LANES), jnp.int32),
        jax.ShapeDtypeStruct((N_TOKENS, LANES), jnp.float32),
        jax.ShapeDtypeStruct((SUBLANES, LANES), jnp.float32),
    ]
    return pl.pallas_call(
        _mixer_kernel,
        grid_spec=pltpu.PrefetchScalarGridSpec(
            num_scalar_prefetch=1,
            grid=(BATCH, n_j),
            in_specs=in_specs,
            out_specs=out_specs,
            scratch_shapes=[
                pltpu.VMEM((ATT_BLOCK, KV_WIDTH), jnp.float32),
                pltpu.VMEM((ATT_BLOCK, KV_WIDTH), jnp.float32),
                pltpu.VMEM((SUBLANES, LANES), jnp.float32),
            ]),
        out_shape=out_shape,
        compiler_params=pltpu.CompilerParams(
            dimension_semantics=("arbitrary", "arbitrary"), vmem_limit_bytes=VMEM_LIMIT_BYTES),
        name="mixer_router",
    )(p["sinks"], x, mod.reshape(BATCH, 1, 6 * D_MODEL), p["g_pre_mix"], p["g_post_mix"],
      p["w_in"], p["b_in"], _attention_bias(), p["sg_ln_g"], p["sg_ln_b"], p["sg_w"], p["sg_b"],
      p["w_br_attn"], p["w_br_sg"], p["w_out"], p["b_out"], p["g_pre_ffn"],
      p["w_router"], p["b_router"])


def _row_copy(src, src_row, dst, dst_row, sem):
    return pltpu.make_async_copy(
        src.at[pl.ds(pl.multiple_of(src_row * ROW_TILES, ROW_TILES), ROW_TILES)],
        dst.at[pl.ds(pl.multiple_of(dst_row * ROW_TILES, ROW_TILES), ROW_TILES)], sem)


def _wait_rows(src, dst, n_rows, sem):
    pltpu.make_async_copy(src.at[pl.ds(0, n_rows * ROW_TILES)],
                          dst.at[pl.ds(0, n_rows * ROW_TILES)], sem).wait()


def _rows_to_matrix(rows_ref, first_row, n_rows):
    return jnp.concatenate(
        [rows_ref[pl.ds(first_row * ROW_TILES + s, n_rows, stride=ROW_TILES), :]
         for s in range(ROW_TILES)], axis=1)


def _dispatch_kernel(slot_ref, pad_ref, h2_ref, xs_hbm, sem):
    def issue(g, carry):
        for r in range(DMA_UNROLL):
            a = g * DMA_UNROLL + r
            tok = g * (DMA_UNROLL // TOP_K) + r // TOP_K
            _row_copy(h2_ref, tok, xs_hbm, slot_ref[0, 0, a], sem).start(priority=r % 2)
        return carry

    lax.fori_loop(0, DSP_TILE * TOP_K // DMA_UNROLL, issue, 0)

    def issue_pad(g, carry):
        for r in range(DMA_UNROLL):
            _row_copy(h2_ref, 0, xs_hbm, pad_ref[0, 0, g * DMA_UNROLL + r], sem).start(priority=r % 2)
        return carry

    lax.fori_loop(0, DSP_PADS // DMA_UNROLL, issue_pad, 0)
    n_rows = (DSP_TILE * TOP_K + DSP_PADS) * ROW_TILES
    pltpu.make_async_copy(xs_hbm.at[pl.ds(0, n_rows)], xs_hbm.at[pl.ds(n_rows, n_rows)], sem).wait()


def _dispatch(slot, pad_slots, h2_rows):
    n_steps = N_TOKENS // DSP_TILE
    return pl.pallas_call(
        _dispatch_kernel,
        grid=(n_steps,),
        in_specs=[
            pl.BlockSpec((1, 1, DSP_TILE * TOP_K), lambda i: (i, 0, 0), memory_space=pltpu.SMEM),
            pl.BlockSpec((1, 1, DSP_PADS), lambda i: (i, 0, 0), memory_space=pltpu.SMEM),
            pl.BlockSpec((DSP_TILE * ROW_TILES, LANES), lambda i: (i, 0)),
        ],
        out_specs=pl.BlockSpec(memory_space=pl.ANY),
        out_shape=jax.ShapeDtypeStruct((N_SLOTS * ROW_TILES, LANES), jnp.float32),
        scratch_shapes=[pltpu.SemaphoreType.DMA],
        compiler_params=pltpu.CompilerParams(dimension_semantics=("arbitrary",)),
        name="dispatch",
    )(slot.reshape(n_steps, 1, DSP_TILE * TOP_K), pad_slots.reshape(n_steps, 1, DSP_PADS), h2_rows)


def _expert_kernel(be_ref, nused_ref, xs_ref, w1_ref, b1_ref, w2_ref, b2_ref, ys_ref, w1b_ref, w2b_ref):
    i = pl.program_id(0)
    used = i < nused_ref[0]

    @pl.when(used & ((i == 0) | (be_ref[i] != be_ref[jnp.maximum(i - 1, 0)])))
    def _():
        w1b_ref[...] = _bf16(w1_ref[0])
        w2b_ref[...] = _bf16(w2_ref[0])

    @pl.when(used)
    def _():
        xb = _bf16(_rows_to_matrix(xs_ref, 0, MOE_BLOCK))
        hb = _dot(xb, w1b_ref[...]) + b1_ref[0]
        g = jnp.minimum(hb[:, :D_EXPERT], SWIGLU_LIMIT)
        lin = jnp.clip(hb[:, D_EXPERT:], -SWIGLU_LIMIT, SWIGLU_LIMIT)
        act = g * _sigmoid(SWIGLU_ALPHA * g) * (lin + 1.0)
        y = _dot(_bf16(act), w2b_ref[...]) + b2_ref[0]
        for s in range(ROW_TILES):
            ys_ref[pl.ds(s, MOE_BLOCK, stride=ROW_TILES), :] = y[:, s * LANES:(s + 1) * LANES]

    @pl.when(jnp.logical_not(used))
    def _():
        ys_ref[...] = jnp.zeros_like(ys_ref)


def _experts(block_e, n_used, xs_rows, w1, b1, w2, b2):
    return pl.pallas_call(
        _expert_kernel,
        grid_spec=pltpu.PrefetchScalarGridSpec(
            num_scalar_prefetch=2,
            grid=(N_MOE_BLOCKS,),
            in_specs=[
                pl.BlockSpec((MOE_BLOCK * ROW_TILES, LANES), lambda i, be, nu: (i, 0)),
                pl.BlockSpec((1, D_MODEL, 2 * D_EXPERT), lambda i, be, nu: (be[i], 0, 0)),
                pl.BlockSpec((1, 1, 2 * D_EXPERT), lambda i, be, nu: (be[i], 0, 0)),
                pl.BlockSpec((1, D_EXPERT, D_MODEL), lambda i, be, nu: (be[i], 0, 0)),
                pl.BlockSpec((1, 1, D_MODEL), lambda i, be, nu: (be[i], 0, 0)),
            ],
            out_specs=pl.BlockSpec((MOE_BLOCK * ROW_TILES, LANES), lambda i, be, nu: (i, 0)),
            scratch_shapes=[
                pltpu.VMEM((D_MODEL, 2 * D_EXPERT), jnp.bfloat16),
                pltpu.VMEM((D_EXPERT, D_MODEL), jnp.bfloat16),
            ],
        ),
        out_shape=jax.ShapeDtypeStruct((N_SLOTS * ROW_TILES, LANES), jnp.float32),
        compiler_params=pltpu.CompilerParams(
            dimension_semantics=("arbitrary",), vmem_limit_bytes=VMEM_LIMIT_BYTES),
        name="experts",
    )(block_e, n_used, xs_rows,
      w1, b1.reshape(N_EXPERTS, 1, 2 * D_EXPERT), w2, b2.reshape(N_EXPERTS, 1, D_MODEL))


def _combine_kernel(slot0_ref, slot_next_ref, ys_hbm, x1_ref, rg_ref, mod_ref, gpost_ref, o_ref,
                    buf_ref, sems):
    tm = CMB_TILE
    n_rows = TOP_K * tm
    i = pl.program_id(0)
    n_steps = pl.num_programs(0)

    def gather(slot_ref, half):
        def issue(g, carry):
            for r in range(DMA_UNROLL):
                tok = g * (DMA_UNROLL // TOP_K) + r // TOP_K
                _row_copy(ys_hbm, slot_ref[0, 0, g * DMA_UNROLL + r], buf_ref.at[half],
                          (r % TOP_K) * tm + tok, sems.at[half]).start(priority=r % 2)
            return carry
        lax.fori_loop(0, n_rows // DMA_UNROLL, issue, 0)

    @pl.when(i == 0)
    def _():
        gather(slot0_ref, 0)

    @pl.when(i + 1 < n_steps)
    def _():
        gather(slot_next_ref, (i + 1) % 2)

    cur = i % 2
    _wait_rows(ys_hbm, buf_ref.at[cur], n_rows, sems.at[cur])
    gates = rg_ref[...]
    rows = buf_ref.at[cur]
    y = jnp.zeros((tm, D_MODEL), jnp.float32)
    for kk in range(TOP_K):
        y = y + _rows_to_matrix(rows, kk * tm, tm) * gates[:, kk:kk + 1]
    gate2 = mod_ref[0][:, 5 * D_MODEL:6 * D_MODEL]
    o_ref[...] = x1_ref[...] + gate2 * _rms(y, gpost_ref[...])


def _combine(slot_tiles, ys_rows, x1, route_g, mod, g_post_ffn):
    tm = CMB_TILE
    n_tiles = N_TOKENS // tm
    tiles_per_seq = SEQ // tm
    return pl.pallas_call(
        _combine_kernel,
        grid=(n_tiles,),
        in_specs=[
            pl.BlockSpec((1, 1, TOP_K * tm), lambda i: (0, 0, 0), memory_space=pltpu.SMEM),
            pl.BlockSpec((1, 1, TOP_K * tm), lambda i: (jnp.minimum(i + 1, n_tiles - 1), 0, 0),
                         memory_space=pltpu.SMEM),
            pl.BlockSpec(memory_space=pl.ANY),
            pl.BlockSpec((tm, D_MODEL), lambda i: (i, 0)),
            pl.BlockSpec((tm, LANES), lambda i: (i, 0)),
            pl.BlockSpec((1, 1, 6 * D_MODEL), lambda i: (i // tiles_per_seq, 0, 0)),
            pl.BlockSpec((1, D_MODEL), lambda i: (0, 0)),
        ],
        out_specs=pl.BlockSpec((tm, D_MODEL), lambda i: (i, 0)),
        out_shape=jax.ShapeDtypeStruct((N_TOKENS, D_MODEL), jnp.float32),
        scratch_shapes=[
            pltpu.VMEM((2, TOP_K * tm * ROW_TILES, LANES), jnp.float32),
            pltpu.SemaphoreType.DMA((2,)),
        ],
        compiler_params=pltpu.CompilerParams(
            dimension_semantics=("arbitrary",), vmem_limit_bytes=VMEM_LIMIT_BYTES),
        name="combine",
    )(slot_tiles, slot_tiles, ys_rows, x1, route_g, mod.reshape(BATCH, 1, 6 * D_MODEL),
      g_post_ffn)


def kernel(x, c, w_mod, b_mod, g_pre_mix, g_post_mix, w_in, b_in, attn_sinks, sg_ln_g, sg_ln_b,
           sg_w, sg_b, w_br_attn, w_br_sg, w_out, b_out, g_pre_ffn, g_post_ffn, w_router,
           b_router, w_mlp1, b_mlp1, w_mlp2, b_mlp2):
    assert x.shape == (BATCH, SEQ, D_MODEL) and w_mod.shape[0] == 1, "single-layer shapes only"
    row = lambda a: a[0].reshape(1, -1)
    bf = lambda a: a[0].astype(jnp.bfloat16)

    mod = _modulation(c, w_mod[0], b_mod[0])

    params = dict(
        sinks=attn_sinks[0], g_pre_mix=row(g_pre_mix), g_post_mix=row(g_post_mix),
        w_in=bf(w_in), b_in=row(b_in), sg_ln_g=row(sg_ln_g), sg_ln_b=row(sg_ln_b), sg_w=sg_w[0],
        sg_b=jnp.repeat(sg_b[0].T, SG_GROUP_DIM, axis=1),
        w_br_attn=bf(w_br_attn), w_br_sg=bf(w_br_sg), w_out=bf(w_out), b_out=row(b_out),
        g_pre_ffn=row(g_pre_ffn),
        w_router=jnp.pad(bf(w_router), ((0, 0), (0, LANES - N_EXPERTS))),
        b_router=jnp.pad(row(b_router), ((0, 0), (0, LANES - N_EXPERTS)), constant_values=_NEG_INF),
    )
    x1, h2_rows, route_i, route_g, counts = _mixer(x, mod, params)

    counts = counts[0, :N_EXPERTS].astype(jnp.int32)
    padded = (counts + MOE_BLOCK - 1) // MOE_BLOCK * MOE_BLOCK
    pends = jnp.cumsum(padded)
    pstarts = pends - padded
    block_start = jnp.arange(N_MOE_BLOCKS, dtype=jnp.int32) * MOE_BLOCK
    block_e = jnp.minimum(jnp.sum((pends[None, :] <= block_start[:, None]).astype(jnp.int32), axis=1),
                          N_EXPERTS - 1)
    expert_ids = jnp.arange(N_EXPERTS, dtype=jnp.int32)
    top_idx = route_i[:, :TOP_K]
    slot = (jnp.sum(jnp.where(top_idx[:, :, None] == expert_ids, pstarts, 0), axis=-1)
            + route_i[:, TOP_K:2 * TOP_K])
    free_start = jnp.concatenate([pstarts + counts, pends[-1:]])
    free_len = jnp.concatenate([padded - counts, N_SLOTS - pends[-1:]])
    free_end = jnp.cumsum(free_len)
    q = jnp.arange(N_PAD_SLOTS, dtype=jnp.int32)
    region = jnp.sum((free_end[None, :] <= q[:, None]).astype(jnp.int32), axis=1)
    region_hit = region[:, None] == jnp.arange(N_EXPERTS + 1, dtype=jnp.int32)
    pad_slots = q + jnp.sum(jnp.where(region_hit, free_start - (free_end - free_len), 0), axis=1)

    xs_rows = _dispatch(slot, pad_slots, h2_rows)
    n_used = (pends[-1:] // MOE_BLOCK).astype(jnp.int32)
    ys_rows = _experts(block_e, n_used, xs_rows, w_mlp1[0], b_mlp1[0], w_mlp2[0], b_mlp2[0])

    slot_tiles = slot.reshape(N_TOKENS // CMB_TILE, 1, TOP_K * CMB_TILE)
    out = _combine(slot_tiles, ys_rows, x1.reshape(N_TOKENS, D_MODEL), route_g, mod, row(g_post_ffn))
    return out.reshape(BATCH, SEQ, D_MODEL)
```

```python
import jax
import jax.numpy as jnp
import numpy as np
from jax import lax
from jax.experimental import pallas as pl
from jax.experimental.pallas import tpu as pltpu

D_MODEL = 1024
BATCH = 16
SEQ = 2048
N_TOKENS = BATCH * SEQ
HEAD_DIM = 64
N_Q_HEADS = 8
N_KV_HEADS = 2
Q_PER_KV = N_Q_HEADS // N_KV_HEADS
ATTN_WIDTH = N_Q_HEADS * HEAD_DIM
KV_WIDTH = N_KV_HEADS * HEAD_DIM
ATT_BLOCK = 128
N_SG_GROUPS = 8
SG_GROUP_DIM = 64
SG_WIDTH = N_SG_GROUPS * SG_GROUP_DIM
N_EXPERTS = 32
TOP_K = 4
D_EXPERT = D_MODEL
SWIGLU_LIMIT = 7.0
SWIGLU_ALPHA = 1.702
MOE_BLOCK = 512
NORM_EPS = 1e-6
N_ASSIGN = N_TOKENS * TOP_K
N_MOE_BLOCKS = N_ASSIGN // MOE_BLOCK + N_EXPERTS
N_SLOTS = N_MOE_BLOCKS * MOE_BLOCK

_Q0 = 0
_K0 = _Q0 + ATTN_WIDTH
_V0 = _K0 + KV_WIDTH
_U0 = _V0 + KV_WIDTH
_SV0 = _U0 + SG_WIDTH
_GA0 = _SV0 + SG_WIDTH
_GS0 = _GA0 + D_MODEL
IN_WIDTH = _GS0 + D_MODEL

LANES = 128
SUBLANES = 8
ROW_TILES = D_MODEL // LANES
assert ROW_TILES == SUBLANES
VMEM_LIMIT_BYTES = 56 * 1024 * 1024

MIX_TILE = 1024
CMB_TILE = 512
DSP_TILE = 256
DMA_UNROLL = 8
N_PAD_SLOTS = N_SLOTS - N_ASSIGN
DSP_PADS = N_PAD_SLOTS // (N_TOKENS // DSP_TILE)
assert DMA_UNROLL % TOP_K == 0 and DSP_PADS % DMA_UNROLL == 0

_NEG_INF = float("-inf")


def _rms(x, g):
    return x * lax.rsqrt(jnp.mean(x * x, axis=-1, keepdims=True) + NORM_EPS) * g


def _bf16(x):
    return x.astype(jnp.bfloat16)


def _sigmoid(x):
    return 0.5 * jnp.tanh(0.5 * x) + 0.5


def _dot(a, b):
    return jnp.dot(a, b, preferred_element_type=jnp.float32)


def _mod_kernel(c_ref, w_ref, b_ref, o_ref):
    c = c_ref[...]
    c_act = c * jax.nn.sigmoid(c)
    o_ref[...] = _dot(_bf16(c_act), _bf16(w_ref[...])) + b_ref[...]


def _modulation(c, w_mod, b_mod):
    n_out = w_mod.shape[1]
    tn = D_MODEL
    return pl.pallas_call(
        _mod_kernel,
        grid=(n_out // tn,),
        in_specs=[
            pl.BlockSpec((BATCH, D_MODEL), lambda n: (0, 0)),
            pl.BlockSpec((D_MODEL, tn), lambda n: (0, n)),
            pl.BlockSpec((1, tn), lambda n: (0, n)),
        ],
        out_specs=pl.BlockSpec((BATCH, tn), lambda n: (0, n)),
        out_shape=jax.ShapeDtypeStruct((BATCH, n_out), jnp.float32),
        compiler_params=pltpu.CompilerParams(dimension_semantics=("arbitrary",)),
        name="modulation",
    )(c, w_mod, b_mod.reshape(1, n_out))


def _attention_bias():
    slopes = 2.0 ** (-8.0 * np.arange(1, N_Q_HEADS + 1) / N_Q_HEADS)
    qi = np.arange(ATT_BLOCK)[:, None]
    kj = np.arange(2 * ATT_BLOCK)[None, :]
    dist = qi + ATT_BLOCK - kj
    in_win = (dist >= 0) & (dist < ATT_BLOCK)
    bias = np.where(in_win[None], -slopes[:, None, None] * dist[None].astype(np.float64), -np.inf)
    bias = bias.reshape(N_KV_HEADS, Q_PER_KV * ATT_BLOCK, 2 * ATT_BLOCK)
    return jnp.asarray(bias, dtype=jnp.float32)


def _mixer_kernel(sinks_ref, x_ref, mod_ref, gpre_ref, gpost_ref, win_ref, bin_ref, abias_ref,
                  lng_ref, lnb_ref, sgw_ref, sgb_ref, wba_ref, wbs_ref, wout_ref, bout_ref,
                  gffn_ref, wr_ref, br_ref,
                  x1_ref, h2_ref, ri_ref, rg_ref, cnt_ref,
                  kprev_ref, vprev_ref, carry_ref):
    b = pl.program_id(0)
    j = pl.program_id(1)
    tm = MIX_TILE
    n_blk = tm // ATT_BLOCK

    @pl.when((b == 0) & (j == 0))
    def _():
        carry_ref[...] = jnp.zeros_like(carry_ref)

    @pl.when(j == 0)
    def _():
        kprev_ref[...] = jnp.zeros_like(kprev_ref)
        vprev_ref[...] = jnp.zeros_like(vprev_ref)

    x = x_ref[0]
    mod = mod_ref[0]
    shift1, scale1, gate1, shift2, scale2, gate2 = (
        mod[:, i * D_MODEL:(i + 1) * D_MODEL] for i in range(6))

    h = _bf16(_rms(x, gpre_ref[...]) * (1.0 + scale1) + shift1)

    def proj(lo, width):
        return _dot(h, win_ref[:, lo:lo + width]) + bin_ref[:, lo:lo + width]

    q = proj(_Q0, ATTN_WIDTH)
    k = proj(_K0, KV_WIDTH)
    v = proj(_V0, KV_WIDTH)
    row_group = lax.broadcasted_iota(jnp.int32, (Q_PER_KV * ATT_BLOCK, 1), 0) // ATT_BLOCK
    col = lax.broadcasted_iota(jnp.int32, (Q_PER_KV * ATT_BLOCK, 2 * ATT_BLOCK), 1)
    y_att_blocks = []
    for i in range(n_blk):
        rows = slice(i * ATT_BLOCK, (i + 1) * ATT_BLOCK)
        if i == 0:
            k_prev, v_prev = kprev_ref[...], vprev_ref[...]
        else:
            prev = slice((i - 1) * ATT_BLOCK, i * ATT_BLOCK)
            k_prev, v_prev = k[prev], v[prev]
        k_cat = _bf16(jnp.concatenate([k_prev, k[rows]], axis=0))
        v_cat = _bf16(jnp.concatenate([v_prev, v[rows]], axis=0))
        q_blk = _bf16(q[rows])
        heads = [None] * N_Q_HEADS
        for hk in range(N_KV_HEADS):
            kv_cols = slice(hk * HEAD_DIM, (hk + 1) * HEAD_DIM)
            q_stack = jnp.concatenate(
                [q_blk[:, (hk * Q_PER_KV + g) * HEAD_DIM:(hk * Q_PER_KV + g + 1) * HEAD_DIM]
                 for g in range(Q_PER_KV)], axis=0)
            scores = lax.dot_general(q_stack, k_cat[:, kv_cols], (((1,), (1,)), ((), ())),
                                     preferred_element_type=jnp.float32)
            logits = scores * (HEAD_DIM ** -0.5) + abias_ref[hk]
            if i == 0:
                n_masked = jnp.where(j == 0, ATT_BLOCK, 0)
                logits = jnp.where(col < n_masked, _NEG_INF, logits)
            sink = jnp.zeros((Q_PER_KV * ATT_BLOCK, 1), jnp.float32)
            for g in range(Q_PER_KV):
                sink = jnp.where(row_group == g, sinks_ref[hk * Q_PER_KV + g], sink)
            m = jnp.maximum(jnp.max(logits, axis=-1, keepdims=True), sink)
            p = jnp.exp(logits - m)
            probs = p / (jnp.sum(p, axis=-1, keepdims=True) + jnp.exp(sink - m))
            out = _dot(_bf16(probs), v_cat[:, kv_cols])
            for g in range(Q_PER_KV):
                heads[hk * Q_PER_KV + g] = out[g * ATT_BLOCK:(g + 1) * ATT_BLOCK]
        y_att_blocks.append(jnp.concatenate(heads, axis=1))
    kprev_ref[...] = k[tm - ATT_BLOCK:]
    vprev_ref[...] = v[tm - ATT_BLOCK:]
    y_att = jnp.concatenate(y_att_blocks, axis=0)
    branch_att = _dot(_bf16(y_att), wba_ref[...])

    u = jax.nn.gelu(proj(_U0, SG_WIDTH))
    sv = jax.nn.gelu(proj(_SV0, SG_WIDTH))
    mu = jnp.mean(sv, axis=-1, keepdims=True)
    var = jnp.mean(jnp.square(sv - mu), axis=-1, keepdims=True)
    vn = _bf16((sv - mu) * lax.rsqrt(var + NORM_EPS) * lng_ref[...] + lnb_ref[...])
    causal = (lax.broadcasted_iota(jnp.int32, (ATT_BLOCK, ATT_BLOCK), 0)
              >= lax.broadcasted_iota(jnp.int32, (ATT_BLOCK, ATT_BLOCK), 1))
    w_sp = [_bf16(jnp.where(causal, sgw_ref[g], 0.0)) for g in range(N_SG_GROUPS)]
    mixed_chunks = []
    for cidx in range(n_blk):
        rows = slice(cidx * ATT_BLOCK, (cidx + 1) * ATT_BLOCK)
        groups = [_dot(w_sp[g], vn[rows, g * SG_GROUP_DIM:(g + 1) * SG_GROUP_DIM])
                  for g in range(N_SG_GROUPS)]
        mixed_chunks.append(jnp.concatenate(groups, axis=1) + sgb_ref[...])
    y_sg = u * jnp.concatenate(mixed_chunks, axis=0)
    branch_sg = _dot(_bf16(y_sg), wbs_ref[...])

    merged = (_sigmoid(proj(_GA0, D_MODEL)) * branch_att
              + _sigmoid(proj(_GS0, D_MODEL)) * branch_sg)
    y = _dot(_bf16(merged), wout_ref[...]) + bout_ref[...]
    x1 = x + gate1 * _rms(y, gpost_ref[...])
    x1_ref[0] = x1

    h2 = _rms(x1, gffn_ref[...]) * (1.0 + scale2) + shift2
    for s in range(ROW_TILES):
        h2_ref[pl.ds(s, tm, stride=ROW_TILES), :] = h2[:, s * LANES:(s + 1) * LANES]
    logits = _dot(_bf16(h2), wr_ref[...]) + br_ref[...]
    lane = lax.broadcasted_iota(jnp.int32, (tm, LANES), 1).astype(jnp.float32)
    top_val, top_idx = [], []
    selected = jnp.zeros((tm, LANES), jnp.float32)
    for _ in range(TOP_K):
        mval = jnp.max(logits, axis=-1, keepdims=True)
        midx = jnp.min(jnp.where(logits == mval, lane, float(LANES)), axis=-1, keepdims=True)
        hit = lane == midx
        selected = jnp.where(hit, 1.0, selected)
        logits = jnp.where(hit, _NEG_INF, logits)
        top_val.append(mval)
        top_idx.append(midx)
    expv = [jnp.exp(tv - top_val[0]) for tv in top_val]
    denom = expv[0] + expv[1] + expv[2] + expv[3]
    strict_lower = _bf16(jnp.where(
        lax.broadcasted_iota(jnp.int32, (ATT_BLOCK, ATT_BLOCK), 0)
        > lax.broadcasted_iota(jnp.int32, (ATT_BLOCK, ATT_BLOCK), 1), 1.0, 0.0))
    total = carry_ref[0:1, :]
    before_blocks = []
    for i in range(n_blk):
        sel_blk = selected[i * ATT_BLOCK:(i + 1) * ATT_BLOCK]
        before_blocks.append(_dot(strict_lower, _bf16(sel_blk)) + total)
        total = total + jnp.sum(sel_blk, axis=0, keepdims=True)
    before = jnp.concatenate(before_blocks, axis=0)
    route_i = jnp.zeros((tm, LANES), jnp.float32)
    route_g = jnp.zeros((tm, LANES), jnp.float32)
    for kk in range(TOP_K):
        rank = jnp.sum(jnp.where(lane == top_idx[kk], before, 0.0), axis=-1, keepdims=True)
        route_i = jnp.where(lane == float(kk), top_idx[kk], route_i)
        route_i = jnp.where(lane == float(TOP_K + kk), rank, route_i)
        route_g = jnp.where(lane == float(kk), expv[kk] / denom, route_g)
    ri_ref[...] = route_i.T[0:2 * TOP_K, :].astype(jnp.int32)
    rg_ref[...] = route_g
    carry_ref[...] = jnp.broadcast_to(total, carry_ref.shape)
    cnt_ref[...] = jnp.broadcast_to(total, cnt_ref.shape)


def _mixer(x, mod, p):
    tm = MIX_TILE
    n_j = SEQ // tm
    const2 = lambda shape: pl.BlockSpec(shape, lambda b, j, s: (0, 0), pipeline_mode=pl.Buffered(1))
    const3 = lambda shape: pl.BlockSpec(shape, lambda b, j, s: (0, 0, 0), pipeline_mode=pl.Buffered(1))
    tile_rows = lambda b, j, s: (b * n_j + j, 0)
    in_specs = [
        pl.BlockSpec((1, tm, D_MODEL), lambda b, j, s: (b, j, 0)),
        pl.BlockSpec((1, 1, 6 * D_MODEL), lambda b, j, s: (b, 0, 0)),
        const2((1, D_MODEL)), const2((1, D_MODEL)),
        const2((D_MODEL, IN_WIDTH)), const2((1, IN_WIDTH)),
        const3((N_KV_HEADS, Q_PER_KV * ATT_BLOCK, 2 * ATT_BLOCK)),
        const2((1, SG_WIDTH)), const2((1, SG_WIDTH)),
        const3((N_SG_GROUPS, ATT_BLOCK, ATT_BLOCK)), const2((ATT_BLOCK, SG_WIDTH)),
        const2((ATTN_WIDTH, D_MODEL)), const2((SG_WIDTH, D_MODEL)),
        const2((D_MODEL, D_MODEL)), const2((1, D_MODEL)),
        const2((1, D_MODEL)),
        const2((D_MODEL, LANES)), const2((1, LANES)),
    ]
    out_specs = [
        pl.BlockSpec((1, tm, D_MODEL), lambda b, j, s: (b, j, 0)),
        pl.BlockSpec((tm * ROW_TILES, LANES), tile_rows),
        pl.BlockSpec((2 * TOP_K, tm), lambda b, j, s: (0, b * n_j + j)),
        pl.BlockSpec((tm, LANES), tile_rows),
        pl.BlockSpec((SUBLANES, LANES), lambda b, j, s: (0, 0)),
    ]
    out_shape = [
        jax.ShapeDtypeStruct((BATCH, SEQ, D_MODEL), jnp.float32),
        jax.ShapeDtypeStruct((N_TOKENS * ROW_TILES, LANES), jnp.float32),
        jax.ShapeDtypeStruct((2 * TOP_K, N_TOKENS), jnp.int32),
        jax.ShapeDtypeStruct((N_TOKENS, LANES), jnp.float32),
        jax.ShapeDtypeStruct((SUBLANES, LANES), jnp.float32),
    ]
    return pl.pallas_call(
        _mixer_kernel,
        grid_spec=pltpu.PrefetchScalarGridSpec(
            num_scalar_prefetch=1,
            grid=(BATCH, n_j),
            in_specs=in_specs,
            out_specs=out_specs,
            scratch_shapes=[
                pltpu.VMEM((ATT_BLOCK, KV_WIDTH), jnp.float32),
                pltpu.VMEM((ATT_BLOCK, KV_WIDTH), jnp.float32),
                pltpu.VMEM((SUBLANES, LANES), jnp.float32),
            ]),
        out_shape=out_shape,
        compiler_params=pltpu.CompilerParams(
            dimension_semantics=("arbitrary", "arbitrary"), vmem_limit_bytes=VMEM_LIMIT_BYTES),
        name="mixer_router",
    )(p["sinks"], x, mod.reshape(BATCH, 1, 6 * D_MODEL), p["g_pre_mix"], p["g_post_mix"],
      p["w_in"], p["b_in"], _attention_bias(), p["sg_ln_g"], p["sg_ln_b"], p["sg_w"], p["sg_b"],
      p["w_br_attn"], p["w_br_sg"], p["w_out"], p["b_out"], p["g_pre_ffn"],
      p["w_router"], p["b_router"])


def _row_copy(src, src_row, dst, dst_row, sem):
    return pltpu.make_async_copy(
        src.at[pl.ds(pl.multiple_of(src_row * ROW_TILES, ROW_TILES), ROW_TILES)],
        dst.at[pl.ds(pl.multiple_of(dst_row * ROW_TILES, ROW_TILES), ROW_TILES)], sem)


def _wait_rows(src, dst, n_rows, sem):
    pltpu.make_async_copy(src.at[pl.ds(0, n_rows * ROW_TILES)],
                          dst.at[pl.ds(0, n_rows * ROW_TILES)], sem).wait()


def _rows_to_matrix(rows_ref, first_row, n_rows):
    return jnp.concatenate(
        [rows_ref[pl.ds(first_row * ROW_TILES + s, n_rows, stride=ROW_TILES), :]
         for s in range(ROW_TILES)], axis=1)


def _dispatch_kernel(slot_ref, pad_ref, h2_ref, xs_hbm, sem):
    def issue(g, carry):
        for r in range(DMA_UNROLL):
            tok = g * (DMA_UNROLL // TOP_K) + r // TOP_K
            _row_copy(h2_ref, tok, xs_hbm, slot_ref[r % TOP_K, tok], sem).start(priority=r % 2)
        return carry

    lax.fori_loop(0, DSP_TILE * TOP_K // DMA_UNROLL, issue, 0)

    def issue_pad(g, carry):
        for r in range(DMA_UNROLL):
            _row_copy(h2_ref, 0, xs_hbm, pad_ref[0, 0, g * DMA_UNROLL + r], sem).start(priority=r % 2)
        return carry

    lax.fori_loop(0, DSP_PADS // DMA_UNROLL, issue_pad, 0)
    n_rows = (DSP_TILE * TOP_K + DSP_PADS) * ROW_TILES
    pltpu.make_async_copy(xs_hbm.at[pl.ds(0, n_rows)], xs_hbm.at[pl.ds(n_rows, n_rows)], sem).wait()


def _dispatch(slot, pad_slots, h2_rows):
    n_steps = N_TOKENS // DSP_TILE
    return pl.pallas_call(
        _dispatch_kernel,
        grid=(n_steps,),
        in_specs=[
            pl.BlockSpec((TOP_K, DSP_TILE), lambda i: (0, i), memory_space=pltpu.SMEM),
            pl.BlockSpec((1, 1, DSP_PADS), lambda i: (i, 0, 0), memory_space=pltpu.SMEM),
            pl.BlockSpec((DSP_TILE * ROW_TILES, LANES), lambda i: (i, 0)),
        ],
        out_specs=pl.BlockSpec(memory_space=pl.ANY),
        out_shape=jax.ShapeDtypeStruct((N_SLOTS * ROW_TILES, LANES), jnp.float32),
        scratch_shapes=[pltpu.SemaphoreType.DMA],
        compiler_params=pltpu.CompilerParams(dimension_semantics=("arbitrary",)),
        name="dispatch",
    )(slot, pad_slots.reshape(n_steps, 1, DSP_PADS), h2_rows)


def _expert_kernel(be_ref, nused_ref, xs_ref, w1_ref, b1_ref, w2_ref, b2_ref, ys_ref, w1b_ref, w2b_ref):
    i = pl.program_id(0)
    used = i < nused_ref[0]

    @pl.when(used & ((i == 0) | (be_ref[i] != be_ref[jnp.maximum(i - 1, 0)])))
    def _():
        w1b_ref[...] = _bf16(w1_ref[0])
        w2b_ref[...] = _bf16(w2_ref[0])

    @pl.when(used)
    def _():
        xb = _bf16(_rows_to_matrix(xs_ref, 0, MOE_BLOCK))
        hb = _dot(xb, w1b_ref[...]) + b1_ref[0]
        g = jnp.minimum(hb[:, :D_EXPERT], SWIGLU_LIMIT)
        lin = jnp.clip(hb[:, D_EXPERT:], -SWIGLU_LIMIT, SWIGLU_LIMIT)
        act = g * _sigmoid(SWIGLU_ALPHA * g) * (lin + 1.0)
        y = _dot(_bf16(act), w2b_ref[...]) + b2_ref[0]
        for s in range(ROW_TILES):
            ys_ref[pl.ds(s, MOE_BLOCK, stride=ROW_TILES), :] = y[:, s * LANES:(s + 1) * LANES]

    @pl.when(jnp.logical_not(used))
    def _():
        ys_ref[...] = jnp.zeros_like(ys_ref)


def _experts(block_e, n_used, xs_rows, w1, b1, w2, b2):
    return pl.pallas_call(
        _expert_kernel,
        grid_spec=pltpu.PrefetchScalarGridSpec(
            num_scalar_prefetch=2,
            grid=(N_MOE_BLOCKS,),
            in_specs=[
                pl.BlockSpec((MOE_BLOCK * ROW_TILES, LANES), lambda i, be, nu: (i, 0)),
                pl.BlockSpec((1, D_MODEL, 2 * D_EXPERT), lambda i, be, nu: (be[i], 0, 0)),
                pl.BlockSpec((1, 1, 2 * D_EXPERT), lambda i, be, nu: (be[i], 0, 0)),
                pl.BlockSpec((1, D_EXPERT, D_MODEL), lambda i, be, nu: (be[i], 0, 0)),
                pl.BlockSpec((1, 1, D_MODEL), lambda i, be, nu: (be[i], 0, 0)),
            ],
            out_specs=pl.BlockSpec((MOE_BLOCK * ROW_TILES, LANES), lambda i, be, nu: (i, 0)),
            scratch_shapes=[
                pltpu.VMEM((D_MODEL, 2 * D_EXPERT), jnp.bfloat16),
                pltpu.VMEM((D_EXPERT, D_MODEL), jnp.bfloat16),
            ],
        ),
        out_shape=jax.ShapeDtypeStruct((N_SLOTS * ROW_TILES, LANES), jnp.float32),
        compiler_params=pltpu.CompilerParams(
            dimension_semantics=("arbitrary",), vmem_limit_bytes=VMEM_LIMIT_BYTES),
        name="experts",
    )(block_e, n_used, xs_rows,
      w1, b1.reshape(N_EXPERTS, 1, 2 * D_EXPERT), w2, b2.reshape(N_EXPERTS, 1, D_MODEL))


def _combine_kernel(slot0_ref, slot_next_ref, ys_hbm, x1_ref, rg_ref, mod_ref, gpost_ref, o_ref,
                    buf_ref, sems):
    tm = CMB_TILE
    n_rows = TOP_K * tm
    i = pl.program_id(0)
    n_steps = pl.num_programs(0)

    def gather(slot_ref, half):
        def issue(g, carry):
            for r in range(DMA_UNROLL):
                tok = g * (DMA_UNROLL // TOP_K) + r // TOP_K
                _row_copy(ys_hbm, slot_ref[r % TOP_K, tok], buf_ref.at[half],
                          (r % TOP_K) * tm + tok, sems.at[half]).start(priority=r % 2)
            return carry
        lax.fori_loop(0, n_rows // DMA_UNROLL, issue, 0)

    @pl.when(i == 0)
    def _():
        gather(slot0_ref, 0)

    @pl.when(i + 1 < n_steps)
    def _():
        gather(slot_next_ref, (i + 1) % 2)

    cur = i % 2
    _wait_rows(ys_hbm, buf_ref.at[cur], n_rows, sems.at[cur])
    gates = rg_ref[...]
    rows = buf_ref.at[cur]
    y = jnp.zeros((tm, D_MODEL), jnp.float32)
    for kk in range(TOP_K):
        y = y + _rows_to_matrix(rows, kk * tm, tm) * gates[:, kk:kk + 1]
    gate2 = mod_ref[0][:, 5 * D_MODEL:6 * D_MODEL]
    o_ref[...] = x1_ref[...] + gate2 * _rms(y, gpost_ref[...])


def _combine(slot, ys_rows, x1, route_g, mod, g_post_ffn):
    tm = CMB_TILE
    n_tiles = N_TOKENS // tm
    tiles_per_seq = SEQ // tm
    return pl.pallas_call(
        _combine_kernel,
        grid=(n_tiles,),
        in_specs=[
            pl.BlockSpec((TOP_K, tm), lambda i: (0, 0), memory_space=pltpu.SMEM),
            pl.BlockSpec((TOP_K, tm), lambda i: (0, jnp.minimum(i + 1, n_tiles - 1)),
                         memory_space=pltpu.SMEM),
            pl.BlockSpec(memory_space=pl.ANY),
            pl.BlockSpec((tm, D_MODEL), lambda i: (i, 0)),
            pl.BlockSpec((tm, LANES), lambda i: (i, 0)),
            pl.BlockSpec((1, 1, 6 * D_MODEL), lambda i: (i // tiles_per_seq, 0, 0)),
            pl.BlockSpec((1, D_MODEL), lambda i: (0, 0)),
        ],
        out_specs=pl.BlockSpec((tm, D_MODEL), lambda i: (i, 0)),
        out_shape=jax.ShapeDtypeStruct((N_TOKENS, D_MODEL), jnp.float32),
        scratch_shapes=[
            pltpu.VMEM((2, TOP_K * tm * ROW_TILES, LANES), jnp.float32),
            pltpu.SemaphoreType.DMA((2,)),
        ],
        compiler_params=pltpu.CompilerParams(
            dimension_semantics=("arbitrary",), vmem_limit_bytes=VMEM_LIMIT_BYTES),
        name="combine",
    )(slot, slot, ys_rows, x1, route_g, mod.reshape(BATCH, 1, 6 * D_MODEL),
      g_post_ffn)


def kernel(x, c, w_mod, b_mod, g_pre_mix, g_post_mix, w_in, b_in, attn_sinks, sg_ln_g, sg_ln_b,
           sg_w, sg_b, w_br_attn, w_br_sg, w_out, b_out, g_pre_ffn, g_post_ffn, w_router,
           b_router, w_mlp1, b_mlp1, w_mlp2, b_mlp2):
    assert x.shape == (BATCH, SEQ, D_MODEL) and w_mod.shape[0] == 1, "single-layer shapes only"
    row = lambda a: a[0].reshape(1, -1)
    bf = lambda a: a[0].astype(jnp.bfloat16)

    mod = _modulation(c, w_mod[0], b_mod[0])

    params = dict(
        sinks=attn_sinks[0], g_pre_mix=row(g_pre_mix), g_post_mix=row(g_post_mix),
        w_in=bf(w_in), b_in=row(b_in), sg_ln_g=row(sg_ln_g), sg_ln_b=row(sg_ln_b), sg_w=sg_w[0],
        sg_b=jnp.repeat(sg_b[0].T, SG_GROUP_DIM, axis=1),
        w_br_attn=bf(w_br_attn), w_br_sg=bf(w_br_sg), w_out=bf(w_out), b_out=row(b_out),
        g_pre_ffn=row(g_pre_ffn),
        w_router=jnp.pad(bf(w_router), ((0, 0), (0, LANES - N_EXPERTS))),
        b_router=jnp.pad(row(b_router), ((0, 0), (0, LANES - N_EXPERTS)), constant_values=_NEG_INF),
    )
    x1, h2_rows, route_i, route_g, counts = _mixer(x, mod, params)

    counts = counts[0, :N_EXPERTS].astype(jnp.int32)
    padded = (counts + MOE_BLOCK - 1) // MOE_BLOCK * MOE_BLOCK
    pends = jnp.cumsum(padded)
    pstarts = pends - padded
    block_start = jnp.arange(N_MOE_BLOCKS, dtype=jnp.int32) * MOE_BLOCK
    block_e = jnp.minimum(jnp.sum((pends[None, :] <= block_start[:, None]).astype(jnp.int32), axis=1),
                          N_EXPERTS - 1)
    expert_ids = jnp.arange(N_EXPERTS, dtype=jnp.int32)
    top_idx = route_i[:TOP_K]
    slot = (jnp.sum(jnp.where(top_idx[:, :, None] == expert_ids, pstarts, 0), axis=-1)
            + route_i[TOP_K:2 * TOP_K])
    free_start = jnp.concatenate([pstarts + counts, pends[-1:]])
    free_len = jnp.concatenate([padded - counts, N_SLOTS - pends[-1:]])
    free_end = jnp.cumsum(free_len)
    q = jnp.arange(N_PAD_SLOTS, dtype=jnp.int32)
    region = jnp.sum((free_end[None, :] <= q[:, None]).astype(jnp.int32), axis=1)
    region_hit = region[:, None] == jnp.arange(N_EXPERTS + 1, dtype=jnp.int32)
    pad_slots = q + jnp.sum(jnp.where(region_hit, free_start - (free_end - free_len), 0), axis=1)

    xs_rows = _dispatch(slot, pad_slots, h2_rows)
    n_used = (pends[-1:] // MOE_BLOCK).astype(jnp.int32)
    ys_rows = _experts(block_e, n_used, xs_rows, w_mlp1[0], b_mlp1[0], w_mlp2[0], b_mlp2[0])

    out = _combine(slot, ys_rows, x1.reshape(N_TOKENS, D_MODEL), route_g, mod, row(g_post_ffn))
    return out.reshape(BATCH, SEQ, D_MODEL)
```

```python
import jax
import jax.numpy as jnp
import numpy as np
from jax import lax
from jax.experimental import pallas as pl
from jax.experimental.pallas import tpu as pltpu

D_MODEL = 1024
BATCH = 16
SEQ = 2048
N_TOKENS = BATCH * SEQ
HEAD_DIM = 64
N_Q_HEADS = 8
N_KV_HEADS = 2
Q_PER_KV = N_Q_HEADS // N_KV_HEADS
ATTN_WIDTH = N_Q_HEADS * HEAD_DIM
KV_WIDTH = N_KV_HEADS * HEAD_DIM
ATT_BLOCK = 128
N_SG_GROUPS = 8
SG_GROUP_DIM = 64
SG_WIDTH = N_SG_GROUPS * SG_GROUP_DIM
N_EXPERTS = 32
TOP_K = 4
D_EXPERT = D_MODEL
SWIGLU_LIMIT = 7.0
SWIGLU_ALPHA = 1.702
MOE_BLOCK = 512
NORM_EPS = 1e-6
N_ASSIGN = N_TOKENS * TOP_K
N_MOE_BLOCKS = N_ASSIGN // MOE_BLOCK + N_EXPERTS
N_SLOTS = N_MOE_BLOCKS * MOE_BLOCK

_Q0 = 0
_K0 = _Q0 + ATTN_WIDTH
_V0 = _K0 + KV_WIDTH
_U0 = _V0 + KV_WIDTH
_SV0 = _U0 + SG_WIDTH
_GA0 = _SV0 + SG_WIDTH
_GS0 = _GA0 + D_MODEL
IN_WIDTH = _GS0 + D_MODEL

LANES = 128
SUBLANES = 8
ROW_TILES = D_MODEL // LANES
assert ROW_TILES == SUBLANES
VMEM_LIMIT_BYTES = 56 * 1024 * 1024

MIX_TILE = 1024
CMB_TILE = 512
DSP_TILE = 256
DMA_UNROLL = 8
N_PAD_SLOTS = N_SLOTS - N_ASSIGN
DSP_PADS = N_PAD_SLOTS // (N_TOKENS // DSP_TILE)
assert DMA_UNROLL % TOP_K == 0 and DSP_PADS % DMA_UNROLL == 0

_NEG_INF = float("-inf")


def _rms(x, g):
    return x * lax.rsqrt(jnp.mean(x * x, axis=-1, keepdims=True) + NORM_EPS) * g


def _bf16(x):
    return x.astype(jnp.bfloat16)


def _sigmoid(x):
    return 0.5 * jnp.tanh(0.5 * x) + 0.5


def _dot(a, b):
    return jnp.dot(a, b, preferred_element_type=jnp.float32)


def _mod_kernel(c_ref, w_ref, b_ref, o_ref):
    c = c_ref[...]
    c_act = c * jax.nn.sigmoid(c)
    o_ref[...] = _dot(_bf16(c_act), _bf16(w_ref[...])) + b_ref[...]


def _modulation(c, w_mod, b_mod):
    n_out = w_mod.shape[1]
    tn = D_MODEL
    return pl.pallas_call(
        _mod_kernel,
        grid=(n_out // tn,),
        in_specs=[
            pl.BlockSpec((BATCH, D_MODEL), lambda n: (0, 0)),
            pl.BlockSpec((D_MODEL, tn), lambda n: (0, n)),
            pl.BlockSpec((1, tn), lambda n: (0, n)),
        ],
        out_specs=pl.BlockSpec((BATCH, tn), lambda n: (0, n)),
        out_shape=jax.ShapeDtypeStruct((BATCH, n_out), jnp.float32),
        compiler_params=pltpu.CompilerParams(dimension_semantics=("arbitrary",)),
        name="modulation",
    )(c, w_mod, b_mod.reshape(1, n_out))


def _attention_bias():
    slopes = 2.0 ** (-8.0 * np.arange(1, N_Q_HEADS + 1) / N_Q_HEADS)
    qi = np.arange(ATT_BLOCK)[:, None]
    kj = np.arange(2 * ATT_BLOCK)[None, :]
    dist = qi + ATT_BLOCK - kj
    in_win = (dist >= 0) & (dist < ATT_BLOCK)
    bias = np.where(in_win[None], -slopes[:, None, None] * dist[None].astype(np.float64), -np.inf)
    bias = bias.reshape(N_KV_HEADS, Q_PER_KV * ATT_BLOCK, 2 * ATT_BLOCK)
    return jnp.asarray(bias, dtype=jnp.float32)


def _mixer_kernel(sinks_ref, x_ref, mod_ref, gpre_ref, gpost_ref, win_ref, bin_ref, abias_ref,
                  lng_ref, lnb_ref, sgw_ref, sgb_ref, wba_ref, wbs_ref, wout_ref, bout_ref,
                  gffn_ref, wr_ref, br_ref,
                  x1_ref, h2_ref, ri_ref, rg_ref, cnt_ref,
                  kprev_ref, vprev_ref, carry_ref):
    b = pl.program_id(0)
    j = pl.program_id(1)
    tm = MIX_TILE
    n_blk = tm // ATT_BLOCK

    @pl.when((b == 0) & (j == 0))
    def _():
        carry_ref[...] = jnp.zeros_like(carry_ref)

    @pl.when(j == 0)
    def _():
        kprev_ref[...] = jnp.zeros_like(kprev_ref)
        vprev_ref[...] = jnp.zeros_like(vprev_ref)

    x = x_ref[0]
    mod = mod_ref[0]
    shift1, scale1, gate1, shift2, scale2, gate2 = (
        mod[:, i * D_MODEL:(i + 1) * D_MODEL] for i in range(6))

    h = _bf16(_rms(x, gpre_ref[...]) * (1.0 + scale1) + shift1)

    def proj(lo, width):
        return _dot(h, win_ref[:, lo:lo + width]) + bin_ref[:, lo:lo + width]

    q = proj(_Q0, ATTN_WIDTH)
    k = proj(_K0, KV_WIDTH)
    v = proj(_V0, KV_WIDTH)
    row_group = lax.broadcasted_iota(jnp.int32, (Q_PER_KV * ATT_BLOCK, 1), 0) // ATT_BLOCK
    col = lax.broadcasted_iota(jnp.int32, (Q_PER_KV * ATT_BLOCK, 2 * ATT_BLOCK), 1)
    y_att_blocks = []
    for i in range(n_blk):
        rows = slice(i * ATT_BLOCK, (i + 1) * ATT_BLOCK)
        if i == 0:
            k_prev, v_prev = kprev_ref[...], vprev_ref[...]
        else:
            prev = slice((i - 1) * ATT_BLOCK, i * ATT_BLOCK)
            k_prev, v_prev = k[prev], v[prev]
        k_cat = _bf16(jnp.concatenate([k_prev, k[rows]], axis=0))
        v_cat = _bf16(jnp.concatenate([v_prev, v[rows]], axis=0))
        q_blk = _bf16(q[rows])
        heads = [None] * N_Q_HEADS
        for hk in range(N_KV_HEADS):
            kv_cols = slice(hk * HEAD_DIM, (hk + 1) * HEAD_DIM)
            q_stack = jnp.concatenate(
                [q_blk[:, (hk * Q_PER_KV + g) * HEAD_DIM:(hk * Q_PER_KV + g + 1) * HEAD_DIM]
                 for g in range(Q_PER_KV)], axis=0)
            scores = lax.dot_general(q_stack, k_cat[:, kv_cols], (((1,), (1,)), ((), ())),
                                     preferred_element_type=jnp.float32)
            logits = scores * (HEAD_DIM ** -0.5) + abias_ref[hk]
            if i == 0:
                n_masked = jnp.where(j == 0, ATT_BLOCK, 0)
                logits = jnp.where(col < n_masked, _NEG_INF, logits)
            sink = jnp.zeros((Q_PER_KV * ATT_BLOCK, 1), jnp.float32)
            for g in range(Q_PER_KV):
                sink = jnp.where(row_group == g, sinks_ref[hk * Q_PER_KV + g], sink)
            m = jnp.maximum(jnp.max(logits, axis=-1, keepdims=True), sink)
            p = jnp.exp(logits - m)
            probs = p / (jnp.sum(p, axis=-1, keepdims=True) + jnp.exp(sink - m))
            out = _dot(_bf16(probs), v_cat[:, kv_cols])
            for g in range(Q_PER_KV):
                heads[hk * Q_PER_KV + g] = out[g * ATT_BLOCK:(g + 1) * ATT_BLOCK]
        y_att_blocks.append(jnp.concatenate(heads, axis=1))
    kprev_ref[...] = k[tm - ATT_BLOCK:]
    vprev_ref[...] = v[tm - ATT_BLOCK:]
    y_att = jnp.concatenate(y_att_blocks, axis=0)
    branch_att = _dot(_bf16(y_att), wba_ref[...])

    u = jax.nn.gelu(proj(_U0, SG_WIDTH))
    sv = jax.nn.gelu(proj(_SV0, SG_WIDTH))
    mu = jnp.mean(sv, axis=-1, keepdims=True)
    var = jnp.mean(jnp.square(sv - mu), axis=-1, keepdims=True)
    vn = _bf16((sv - mu) * lax.rsqrt(var + NORM_EPS) * lng_ref[...] + lnb_ref[...])
    causal = (lax.broadcasted_iota(jnp.int32, (ATT_BLOCK, ATT_BLOCK), 0)
              >= lax.broadcasted_iota(jnp.int32, (ATT_BLOCK, ATT_BLOCK), 1))
    w_sp = [_bf16(jnp.where(causal, sgw_ref[g], 0.0)) for g in range(N_SG_GROUPS)]
    mixed_chunks = []
    for cidx in range(n_blk):
        rows = slice(cidx * ATT_BLOCK, (cidx + 1) * ATT_BLOCK)
        groups = [_dot(w_sp[g], vn[rows, g * SG_GROUP_DIM:(g + 1) * SG_GROUP_DIM])
                  for g in range(N_SG_GROUPS)]
        mixed_chunks.append(jnp.concatenate(groups, axis=1) + sgb_ref[...])
    y_sg = u * jnp.concatenate(mixed_chunks, axis=0)
    branch_sg = _dot(_bf16(y_sg), wbs_ref[...])

    merged = (_sigmoid(proj(_GA0, D_MODEL)) * branch_att
              + _sigmoid(proj(_GS0, D_MODEL)) * branch_sg)
    y = _dot(_bf16(merged), wout_ref[...]) + bout_ref[...]
    x1 = x + gate1 * _rms(y, gpost_ref[...])
    x1_ref[0] = x1

    h2 = _rms(x1, gffn_ref[...]) * (1.0 + scale2) + shift2
    for s in range(ROW_TILES):
        h2_ref[pl.ds(s, tm, stride=ROW_TILES), :] = h2[:, s * LANES:(s + 1) * LANES]
    logits = _dot(_bf16(h2), wr_ref[...]) + br_ref[...]
    lane = lax.broadcasted_iota(jnp.int32, (tm, LANES), 1).astype(jnp.float32)
    top_val, top_idx = [], []
    selected = jnp.zeros((tm, LANES), jnp.float32)
    for _ in range(TOP_K):
        mval = jnp.max(logits, axis=-1, keepdims=True)
        midx = jnp.min(jnp.where(logits == mval, lane, float(LANES)), axis=-1, keepdims=True)
        hit = lane == midx
        selected = jnp.where(hit, 1.0, selected)
        logits = jnp.where(hit, _NEG_INF, logits)
        top_val.append(mval)
        top_idx.append(midx)
    expv = [jnp.exp(tv - top_val[0]) for tv in top_val]
    denom = expv[0] + expv[1] + expv[2] + expv[3]
    strict_lower = _bf16(jnp.where(
        lax.broadcasted_iota(jnp.int32, (ATT_BLOCK, ATT_BLOCK), 0)
        > lax.broadcasted_iota(jnp.int32, (ATT_BLOCK, ATT_BLOCK), 1), 1.0, 0.0))
    total = carry_ref[0:1, :]
    before_blocks = []
    for i in range(n_blk):
        sel_blk = selected[i * ATT_BLOCK:(i + 1) * ATT_BLOCK]
        before_blocks.append(_dot(strict_lower, _bf16(sel_blk)) + total)
        total = total + jnp.sum(sel_blk, axis=0, keepdims=True)
    before = jnp.concatenate(before_blocks, axis=0)
    route_i = jnp.zeros((tm, LANES), jnp.float32)
    route_g = jnp.zeros((tm, LANES), jnp.float32)
    for kk in range(TOP_K):
        rank = jnp.sum(jnp.where(lane == top_idx[kk], before, 0.0), axis=-1, keepdims=True)
        route_i = jnp.where(lane == float(kk), top_idx[kk], route_i)
        route_i = jnp.where(lane == float(TOP_K + kk), rank, route_i)
        route_g = jnp.where(lane == float(kk), expv[kk] / denom, route_g)
    ri_ref[...] = route_i.T[0:2 * TOP_K, :].astype(jnp.int32)
    rg_ref[...] = route_g
    carry_ref[...] = jnp.broadcast_to(total, carry_ref.shape)
    cnt_ref[...] = jnp.broadcast_to(total, cnt_ref.shape)


def _mixer(x, mod, p):
    tm = MIX_TILE
    n_j = SEQ // tm
    const2 = lambda shape: pl.BlockSpec(shape, lambda b, j, s: (0, 0), pipeline_mode=pl.Buffered(1))
    const3 = lambda shape: pl.BlockSpec(shape, lambda b, j, s: (0, 0, 0), pipeline_mode=pl.Buffered(1))
    tile_rows = lambda b, j, s: (b * n_j + j, 0)
    in_specs = [
        pl.BlockSpec((1, tm, D_MODEL), lambda b, j, s: (b, j, 0)),
        pl.BlockSpec((1, 1, 6 * D_MODEL), lambda b, j, s: (b, 0, 0)),
        const2((1, D_MODEL)), const2((1, D_MODEL)),
        const2((D_MODEL, IN_WIDTH)), const2((1, IN_WIDTH)),
        const3((N_KV_HEADS, Q_PER_KV * ATT_BLOCK, 2 * ATT_BLOCK)),
        const2((1, SG_WIDTH)), const2((1, SG_WIDTH)),
        const3((N_SG_GROUPS, ATT_BLOCK, ATT_BLOCK)), const2((ATT_BLOCK, SG_WIDTH)),
        const2((ATTN_WIDTH, D_MODEL)), const2((SG_WIDTH, D_MODEL)),
        const2((D_MODEL, D_MODEL)), const2((1, D_MODEL)),
        const2((1, D_MODEL)),
        const2((D_MODEL, LANES)), const2((1, LANES)),
    ]
    out_specs = [
        pl.BlockSpec((1, tm, D_MODEL), lambda b, j, s: (b, j, 0)),
        pl.BlockSpec((tm * ROW_TILES, LANES), tile_rows),
        pl.BlockSpec((2 * TOP_K, tm), lambda b, j, s: (0, b * n_j + j)),
        pl.BlockSpec((tm, LANES), tile_rows),
        pl.BlockSpec((SUBLANES, LANES), lambda b, j, s: (0, 0)),
    ]
    out_shape = [
        jax.ShapeDtypeStruct((BATCH, SEQ, D_MODEL), jnp.float32),
        jax.ShapeDtypeStruct((N_TOKENS * ROW_TILES, LANES), jnp.float32),
        jax.ShapeDtypeStruct((2 * TOP_K, N_TOKENS), jnp.int32),
        jax.ShapeDtypeStruct((N_TOKENS, LANES), jnp.float32),
        jax.ShapeDtypeStruct((SUBLANES, LANES), jnp.float32),
    ]
    return pl.pallas_call(
        _mixer_kernel,
        grid_spec=pltpu.PrefetchScalarGridSpec(
            num_scalar_prefetch=1,
            grid=(BATCH, n_j),
            in_specs=in_specs,
            out_specs=out_specs,
            scratch_shapes=[
                pltpu.VMEM((ATT_BLOCK, KV_WIDTH), jnp.float32),
                pltpu.VMEM((ATT_BLOCK, KV_WIDTH), jnp.float32),
                pltpu.VMEM((SUBLANES, LANES), jnp.float32),
            ]),
        out_shape=out_shape,
        compiler_params=pltpu.CompilerParams(
            dimension_semantics=("arbitrary", "arbitrary"), vmem_limit_bytes=VMEM_LIMIT_BYTES),
        name="mixer_router",
    )(p["sinks"], x, mod.reshape(BATCH, 1, 6 * D_MODEL), p["g_pre_mix"], p["g_post_mix"],
      p["w_in"], p["b_in"], _attention_bias(), p["sg_ln_g"], p["sg_ln_b"], p["sg_w"], p["sg_b"],
      p["w_br_attn"], p["w_br_sg"], p["w_out"], p["b_out"], p["g_pre_ffn"],
      p["w_router"], p["b_router"])


def _row_copy(src, src_row, dst, dst_row, sem):
    return pltpu.make_async_copy(
        src.at[pl.ds(pl.multiple_of(src_row * ROW_TILES, ROW_TILES), ROW_TILES)],
        dst.at[pl.ds(pl.multiple_of(dst_row * ROW_TILES, ROW_TILES), ROW_TILES)], sem)


def _wait_rows(src, dst, n_rows, sem):
    pltpu.make_async_copy(src.at[pl.ds(0, n_rows * ROW_TILES)],
                          dst.at[pl.ds(0, n_rows * ROW_TILES)], sem).wait()


def _tile_kmajor(slot, tile):
    n_tiles = slot.shape[1] // tile
    return slot.reshape(TOP_K, n_tiles, tile).transpose(1, 0, 2).reshape(n_tiles, 1, TOP_K * tile)


def _rows_to_matrix(rows_ref, first_row, n_rows):
    return jnp.concatenate(
        [rows_ref[pl.ds(first_row * ROW_TILES + s, n_rows, stride=ROW_TILES), :]
         for s in range(ROW_TILES)], axis=1)


def _dispatch_kernel(slot_ref, pad_ref, h2_ref, xs_hbm, sem):
    def issue(g, carry):
        for r in range(DMA_UNROLL):
            tok = g * (DMA_UNROLL // TOP_K) + r // TOP_K
            _row_copy(h2_ref, tok, xs_hbm, slot_ref[0, 0, (r % TOP_K) * DSP_TILE + tok],
                      sem).start(priority=r % 2)
        return carry

    lax.fori_loop(0, DSP_TILE * TOP_K // DMA_UNROLL, issue, 0)

    def issue_pad(g, carry):
        for r in range(DMA_UNROLL):
            _row_copy(h2_ref, 0, xs_hbm, pad_ref[0, 0, g * DMA_UNROLL + r], sem).start(priority=r % 2)
        return carry

    lax.fori_loop(0, DSP_PADS // DMA_UNROLL, issue_pad, 0)
    n_rows = (DSP_TILE * TOP_K + DSP_PADS) * ROW_TILES
    pltpu.make_async_copy(xs_hbm.at[pl.ds(0, n_rows)], xs_hbm.at[pl.ds(n_rows, n_rows)], sem).wait()


def _dispatch(slot, pad_slots, h2_rows):
    n_steps = N_TOKENS // DSP_TILE
    return pl.pallas_call(
        _dispatch_kernel,
        grid=(n_steps,),
        in_specs=[
            pl.BlockSpec((1, 1, TOP_K * DSP_TILE), lambda i: (i, 0, 0), memory_space=pltpu.SMEM),
            pl.BlockSpec((1, 1, DSP_PADS), lambda i: (i, 0, 0), memory_space=pltpu.SMEM),
            pl.BlockSpec((DSP_TILE * ROW_TILES, LANES), lambda i: (i, 0)),
        ],
        out_specs=pl.BlockSpec(memory_space=pl.ANY),
        out_shape=jax.ShapeDtypeStruct((N_SLOTS * ROW_TILES, LANES), jnp.float32),
        scratch_shapes=[pltpu.SemaphoreType.DMA],
        compiler_params=pltpu.CompilerParams(dimension_semantics=("arbitrary",)),
        name="dispatch",
    )(_tile_kmajor(slot, DSP_TILE), pad_slots.reshape(n_steps, 1, DSP_PADS), h2_rows)


def _expert_kernel(be_ref, nused_ref, xs_ref, w1_ref, b1_ref, w2_ref, b2_ref, ys_ref, w1b_ref, w2b_ref):
    i = pl.program_id(0)
    used = i < nused_ref[0]

    @pl.when(used & ((i == 0) | (be_ref[i] != be_ref[jnp.maximum(i - 1, 0)])))
    def _():
        w1b_ref[...] = _bf16(w1_ref[0])
        w2b_ref[...] = _bf16(w2_ref[0])

    @pl.when(used)
    def _():
        xb = _bf16(_rows_to_matrix(xs_ref, 0, MOE_BLOCK))
        hb = _dot(xb, w1b_ref[...]) + b1_ref[0]
        g = jnp.minimum(hb[:, :D_EXPERT], SWIGLU_LIMIT)
        lin = jnp.clip(hb[:, D_EXPERT:], -SWIGLU_LIMIT, SWIGLU_LIMIT)
        act = g * _sigmoid(SWIGLU_ALPHA * g) * (lin + 1.0)
        y = _dot(_bf16(act), w2b_ref[...]) + b2_ref[0]
        for s in range(ROW_TILES):
            ys_ref[pl.ds(s, MOE_BLOCK, stride=ROW_TILES), :] = y[:, s * LANES:(s + 1) * LANES]

    @pl.when(jnp.logical_not(used))
    def _():
        ys_ref[...] = jnp.zeros_like(ys_ref)


def _experts(block_e, n_used, xs_rows, w1, b1, w2, b2):
    return pl.pallas_call(
        _expert_kernel,
        grid_spec=pltpu.PrefetchScalarGridSpec(
            num_scalar_prefetch=2,
            grid=(N_MOE_BLOCKS,),
            in_specs=[
                pl.BlockSpec((MOE_BLOCK * ROW_TILES, LANES), lambda i, be, nu: (i, 0)),
                pl.BlockSpec((1, D_MODEL, 2 * D_EXPERT), lambda i, be, nu: (be[i], 0, 0)),
                pl.BlockSpec((1, 1, 2 * D_EXPERT), lambda i, be, nu: (be[i], 0, 0)),
                pl.BlockSpec((1, D_EXPERT, D_MODEL), lambda i, be, nu: (be[i], 0, 0)),
                pl.BlockSpec((1, 1, D_MODEL), lambda i, be, nu: (be[i], 0, 0)),
            ],
            out_specs=pl.BlockSpec((MOE_BLOCK * ROW_TILES, LANES), lambda i, be, nu: (i, 0)),
            scratch_shapes=[
                pltpu.VMEM((D_MODEL, 2 * D_EXPERT), jnp.bfloat16),
                pltpu.VMEM((D_EXPERT, D_MODEL), jnp.bfloat16),
            ],
        ),
        out_shape=jax.ShapeDtypeStruct((N_SLOTS * ROW_TILES, LANES), jnp.float32),
        compiler_params=pltpu.CompilerParams(
            dimension_semantics=("arbitrary",), vmem_limit_bytes=VMEM_LIMIT_BYTES),
        name="experts",
    )(block_e, n_used, xs_rows,
      w1, b1.reshape(N_EXPERTS, 1, 2 * D_EXPERT), w2, b2.reshape(N_EXPERTS, 1, D_MODEL))


def _combine_kernel(slot0_ref, slot_next_ref, ys_hbm, x1_ref, rg_ref, mod_ref, gpost_ref, o_ref,
                    buf_ref, sems):
    tm = CMB_TILE
    n_rows = TOP_K * tm
    i = pl.program_id(0)
    n_steps = pl.num_programs(0)

    def gather(slot_ref, half):
        def issue(g, carry):
            for r in range(DMA_UNROLL):
                tok = g * (DMA_UNROLL // TOP_K) + r // TOP_K
                _row_copy(ys_hbm, slot_ref[0, 0, (r % TOP_K) * tm + tok], buf_ref.at[half],
                          (r % TOP_K) * tm + tok, sems.at[half]).start(priority=r % 2)
            return carry
        lax.fori_loop(0, n_rows // DMA_UNROLL, issue, 0)

    @pl.when(i == 0)
    def _():
        gather(slot0_ref, 0)

    @pl.when(i + 1 < n_steps)
    def _():
        gather(slot_next_ref, (i + 1) % 2)

    cur = i % 2
    _wait_rows(ys_hbm, buf_ref.at[cur], n_rows, sems.at[cur])
    gates = rg_ref[...]
    rows = buf_ref.at[cur]
    y = jnp.zeros((tm, D_MODEL), jnp.float32)
    for kk in range(TOP_K):
        y = y + _rows_to_matrix(rows, kk * tm, tm) * gates[:, kk:kk + 1]
    gate2 = mod_ref[0][:, 5 * D_MODEL:6 * D_MODEL]
    o_ref[...] = x1_ref[...] + gate2 * _rms(y, gpost_ref[...])


def _combine(slot, ys_rows, x1, route_g, mod, g_post_ffn):
    tm = CMB_TILE
    slot_tiles = _tile_kmajor(slot, tm)
    n_tiles = N_TOKENS // tm
    tiles_per_seq = SEQ // tm
    return pl.pallas_call(
        _combine_kernel,
        grid=(n_tiles,),
        in_specs=[
            pl.BlockSpec((1, 1, TOP_K * tm), lambda i: (0, 0, 0), memory_space=pltpu.SMEM),
            pl.BlockSpec((1, 1, TOP_K * tm), lambda i: (jnp.minimum(i + 1, n_tiles - 1), 0, 0),
                         memory_space=pltpu.SMEM),
            pl.BlockSpec(memory_space=pl.ANY),
            pl.BlockSpec((tm, D_MODEL), lambda i: (i, 0)),
            pl.BlockSpec((tm, LANES), lambda i: (i, 0)),
            pl.BlockSpec((1, 1, 6 * D_MODEL), lambda i: (i // tiles_per_seq, 0, 0)),
            pl.BlockSpec((1, D_MODEL), lambda i: (0, 0)),
        ],
        out_specs=pl.BlockSpec((tm, D_MODEL), lambda i: (i, 0)),
        out_shape=jax.ShapeDtypeStruct((N_TOKENS, D_MODEL), jnp.float32),
        scratch_shapes=[
            pltpu.VMEM((2, TOP_K * tm * ROW_TILES, LANES), jnp.float32),
            pltpu.SemaphoreType.DMA((2,)),
        ],
        compiler_params=pltpu.CompilerParams(
            dimension_semantics=("arbitrary",), vmem_limit_bytes=VMEM_LIMIT_BYTES),
        name="combine",
    )(slot_tiles, slot_tiles, ys_rows, x1, route_g, mod.reshape(BATCH, 1, 6 * D_MODEL),
      g_post_ffn)


def kernel(x, c, w_mod, b_mod, g_pre_mix, g_post_mix, w_in, b_in, attn_sinks, sg_ln_g, sg_ln_b,
           sg_w, sg_b, w_br_attn, w_br_sg, w_out, b_out, g_pre_ffn, g_post_ffn, w_router,
           b_router, w_mlp1, b_mlp1, w_mlp2, b_mlp2):
    assert x.shape == (BATCH, SEQ, D_MODEL) and w_mod.shape[0] == 1, "single-layer shapes only"
    row = lambda a: a[0].reshape(1, -1)
    bf = lambda a: a[0].astype(jnp.bfloat16)

    mod = _modulation(c, w_mod[0], b_mod[0])

    params = dict(
        sinks=attn_sinks[0], g_pre_mix=row(g_pre_mix), g_post_mix=row(g_post_mix),
        w_in=bf(w_in), b_in=row(b_in), sg_ln_g=row(sg_ln_g), sg_ln_b=row(sg_ln_b), sg_w=sg_w[0],
        sg_b=jnp.repeat(sg_b[0].T, SG_GROUP_DIM, axis=1),
        w_br_attn=bf(w_br_attn), w_br_sg=bf(w_br_sg), w_out=bf(w_out), b_out=row(b_out),
        g_pre_ffn=row(g_pre_ffn),
        w_router=jnp.pad(bf(w_router), ((0, 0), (0, LANES - N_EXPERTS))),
        b_router=jnp.pad(row(b_router), ((0, 0), (0, LANES - N_EXPERTS)), constant_values=_NEG_INF),
    )
    x1, h2_rows, route_i, route_g, counts = _mixer(x, mod, params)

    counts = counts[0, :N_EXPERTS].astype(jnp.int32)
    padded = (counts + MOE_BLOCK - 1) // MOE_BLOCK * MOE_BLOCK
    pends = jnp.cumsum(padded)
    pstarts = pends - padded
    block_start = jnp.arange(N_MOE_BLOCKS, dtype=jnp.int32) * MOE_BLOCK
    block_e = jnp.minimum(jnp.sum((pends[None, :] <= block_start[:, None]).astype(jnp.int32), axis=1),
                          N_EXPERTS - 1)
    expert_ids = jnp.arange(N_EXPERTS, dtype=jnp.int32)
    top_idx = route_i[:TOP_K]
    slot = (jnp.sum(jnp.where(top_idx[:, :, None] == expert_ids, pstarts, 0), axis=-1)
            + route_i[TOP_K:2 * TOP_K])
    free_start = jnp.concatenate([pstarts + counts, pends[-1:]])
    free_len = jnp.concatenate([padded - counts, N_SLOTS - pends[-1:]])
    free_end = jnp.cumsum(free_len)
    q = jnp.arange(N_PAD_SLOTS, dtype=jnp.int32)
    region = jnp.sum((free_end[None, :] <= q[:, None]).astype(jnp.int32), axis=1)
    region_hit = region[:, None] == jnp.arange(N_EXPERTS + 1, dtype=jnp.int32)
    pad_slots = q + jnp.sum(jnp.where(region_hit, free_start - (free_end - free_len), 0), axis=1)

    xs_rows = _dispatch(slot, pad_slots, h2_rows)
    n_used = (pends[-1:] // MOE_BLOCK).astype(jnp.int32)
    ys_rows = _experts(block_e, n_used, xs_rows, w_mlp1[0], b_mlp1[0], w_mlp2[0], b_mlp2[0])

    out = _combine(slot, ys_rows, x1.reshape(N_TOKENS, D_MODEL), route_g, mod, row(g_post_ffn))
    return out.reshape(BATCH, SEQ, D_MODEL)
```

```python
import jax
import jax.numpy as jnp
import numpy as np
from jax import lax
from jax.experimental import pallas as pl
from jax.experimental.pallas import tpu as pltpu

D_MODEL = 1024
BATCH = 16
SEQ = 2048
N_TOKENS = BATCH * SEQ
HEAD_DIM = 64
N_Q_HEADS = 8
N_KV_HEADS = 2
Q_PER_KV = N_Q_HEADS // N_KV_HEADS
ATTN_WIDTH = N_Q_HEADS * HEAD_DIM
KV_WIDTH = N_KV_HEADS * HEAD_DIM
ATT_BLOCK = 128
N_SG_GROUPS = 8
SG_GROUP_DIM = 64
SG_WIDTH = N_SG_GROUPS * SG_GROUP_DIM
N_EXPERTS = 32
TOP_K = 4
D_EXPERT = D_MODEL
SWIGLU_LIMIT = 7.0
SWIGLU_ALPHA = 1.702
MOE_BLOCK = 512
NORM_EPS = 1e-6
N_ASSIGN = N_TOKENS * TOP_K
N_MOE_BLOCKS = N_ASSIGN // MOE_BLOCK + N_EXPERTS
N_SLOTS = N_MOE_BLOCKS * MOE_BLOCK

_Q0 = 0
_K0 = _Q0 + ATTN_WIDTH
_V0 = _K0 + KV_WIDTH
_U0 = _V0 + KV_WIDTH
_SV0 = _U0 + SG_WIDTH
_GA0 = _SV0 + SG_WIDTH
_GS0 = _GA0 + D_MODEL
IN_WIDTH = _GS0 + D_MODEL

LANES = 128
SUBLANES = 8
ROW_TILES = D_MODEL // LANES
assert ROW_TILES == SUBLANES
VMEM_LIMIT_BYTES = 56 * 1024 * 1024

MIX_TILE = 1024
CMB_TILE = 512
DSP_TILE = 256
DMA_UNROLL = 8
N_PAD_SLOTS = N_SLOTS - N_ASSIGN
DSP_PADS = N_PAD_SLOTS // (N_TOKENS // DSP_TILE)
assert DMA_UNROLL % TOP_K == 0 and DSP_PADS % DMA_UNROLL == 0

_NEG_INF = float("-inf")


def _rms(x, g):
    return x * lax.rsqrt(jnp.mean(x * x, axis=-1, keepdims=True) + NORM_EPS) * g


def _bf16(x):
    return x.astype(jnp.bfloat16)


def _sigmoid(x):
    return 0.5 * jnp.tanh(0.5 * x) + 0.5


def _dot(a, b):
    return jnp.dot(a, b, preferred_element_type=jnp.float32)


def _mod_kernel(c_ref, w_ref, b_ref, o_ref):
    c = c_ref[...]
    c_act = c * jax.nn.sigmoid(c)
    o_ref[...] = _dot(_bf16(c_act), _bf16(w_ref[...])) + b_ref[...]


def _modulation(c, w_mod, b_mod):
    n_out = w_mod.shape[1]
    tn = D_MODEL
    return pl.pallas_call(
        _mod_kernel,
        grid=(n_out // tn,),
        in_specs=[
            pl.BlockSpec((BATCH, D_MODEL), lambda n: (0, 0)),
            pl.BlockSpec((D_MODEL, tn), lambda n: (0, n)),
            pl.BlockSpec((1, tn), lambda n: (0, n)),
        ],
        out_specs=pl.BlockSpec((BATCH, tn), lambda n: (0, n)),
        out_shape=jax.ShapeDtypeStruct((BATCH, n_out), jnp.float32),
        compiler_params=pltpu.CompilerParams(dimension_semantics=("arbitrary",)),
        name="modulation",
    )(c, w_mod, b_mod.reshape(1, n_out))


def _attention_bias():
    slopes = 2.0 ** (-8.0 * np.arange(1, N_Q_HEADS + 1) / N_Q_HEADS)
    qi = np.arange(ATT_BLOCK)[:, None]
    kj = np.arange(2 * ATT_BLOCK)[None, :]
    dist = qi + ATT_BLOCK - kj
    in_win = (dist >= 0) & (dist < ATT_BLOCK)
    bias = np.where(in_win[None], -slopes[:, None, None] * dist[None].astype(np.float64), -np.inf)
    bias = bias.reshape(N_KV_HEADS, Q_PER_KV * ATT_BLOCK, 2 * ATT_BLOCK)
    return jnp.asarray(bias, dtype=jnp.float32)


def _mixer_kernel(sinks_ref, x_ref, mod_ref, gpre_ref, gpost_ref, win_ref, bin_ref, abias_ref,
                  lng_ref, lnb_ref, sgw_ref, sgb_ref, wba_ref, wbs_ref, wout_ref, bout_ref,
                  gffn_ref, wr_ref, br_ref,
                  x1_ref, h2_ref, ri_ref, rg_ref, cnt_ref,
                  kprev_ref, vprev_ref, carry_ref):
    b = pl.program_id(0)
    j = pl.program_id(1)
    tm = MIX_TILE
    n_blk = tm // ATT_BLOCK

    @pl.when((b == 0) & (j == 0))
    def _():
        carry_ref[...] = jnp.zeros_like(carry_ref)

    @pl.when(j == 0)
    def _():
        kprev_ref[...] = jnp.zeros_like(kprev_ref)
        vprev_ref[...] = jnp.zeros_like(vprev_ref)

    x = x_ref[0]
    mod = mod_ref[0]
    shift1, scale1, gate1, shift2, scale2, gate2 = (
        mod[:, i * D_MODEL:(i + 1) * D_MODEL] for i in range(6))

    h = _bf16(_rms(x, gpre_ref[...]) * (1.0 + scale1) + shift1)

    def proj(lo, width):
        return _dot(h, win_ref[:, lo:lo + width]) + bin_ref[:, lo:lo + width]

    q = proj(_Q0, ATTN_WIDTH)
    kv = proj(_K0, 2 * KV_WIDTH)
    k = kv[:, :KV_WIDTH]
    v = kv[:, KV_WIDTH:]
    row_group = lax.broadcasted_iota(jnp.int32, (Q_PER_KV * ATT_BLOCK, 1), 0) // ATT_BLOCK
    col = lax.broadcasted_iota(jnp.int32, (Q_PER_KV * ATT_BLOCK, 2 * ATT_BLOCK), 1)
    y_att_blocks = []
    for i in range(n_blk):
        rows = slice(i * ATT_BLOCK, (i + 1) * ATT_BLOCK)
        if i == 0:
            k_prev, v_prev = kprev_ref[...], vprev_ref[...]
        else:
            prev = slice((i - 1) * ATT_BLOCK, i * ATT_BLOCK)
            k_prev, v_prev = k[prev], v[prev]
        k_cat = _bf16(jnp.concatenate([k_prev, k[rows]], axis=0))
        v_cat = _bf16(jnp.concatenate([v_prev, v[rows]], axis=0))
        q_blk = _bf16(q[rows])
        heads = [None] * N_Q_HEADS
        for hk in range(N_KV_HEADS):
            kv_cols = slice(hk * HEAD_DIM, (hk + 1) * HEAD_DIM)
            q_stack = jnp.concatenate(
                [q_blk[:, (hk * Q_PER_KV + g) * HEAD_DIM:(hk * Q_PER_KV + g + 1) * HEAD_DIM]
                 for g in range(Q_PER_KV)], axis=0)
            scores = lax.dot_general(q_stack, k_cat[:, kv_cols], (((1,), (1,)), ((), ())),
                                     preferred_element_type=jnp.float32)
            logits = scores * (HEAD_DIM ** -0.5) + abias_ref[hk]
            if i == 0:
                n_masked = jnp.where(j == 0, ATT_BLOCK, 0)
                logits = jnp.where(col < n_masked, _NEG_INF, logits)
            sink = jnp.zeros((Q_PER_KV * ATT_BLOCK, 1), jnp.float32)
            for g in range(Q_PER_KV):
                sink = jnp.where(row_group == g, sinks_ref[hk * Q_PER_KV + g], sink)
            m = jnp.maximum(jnp.max(logits, axis=-1, keepdims=True), sink)
            p = jnp.exp(logits - m)
            probs = p / (jnp.sum(p, axis=-1, keepdims=True) + jnp.exp(sink - m))
            out = _dot(_bf16(probs), v_cat[:, kv_cols])
            for g in range(Q_PER_KV):
                heads[hk * Q_PER_KV + g] = out[g * ATT_BLOCK:(g + 1) * ATT_BLOCK]
        y_att_blocks.append(jnp.concatenate(heads, axis=1))
    kprev_ref[...] = k[tm - ATT_BLOCK:]
    vprev_ref[...] = v[tm - ATT_BLOCK:]
    y_att = jnp.concatenate(y_att_blocks, axis=0)
    branch_att = _dot(_bf16(y_att), wba_ref[...])

    u = jax.nn.gelu(proj(_U0, SG_WIDTH))
    sv = jax.nn.gelu(proj(_SV0, SG_WIDTH))
    mu = jnp.mean(sv, axis=-1, keepdims=True)
    var = jnp.mean(jnp.square(sv - mu), axis=-1, keepdims=True)
    vn = _bf16((sv - mu) * lax.rsqrt(var + NORM_EPS) * lng_ref[...] + lnb_ref[...])
    causal = (lax.broadcasted_iota(jnp.int32, (ATT_BLOCK, ATT_BLOCK), 0)
              >= lax.broadcasted_iota(jnp.int32, (ATT_BLOCK, ATT_BLOCK), 1))
    w_sp = [_bf16(jnp.where(causal, sgw_ref[g], 0.0)) for g in range(N_SG_GROUPS)]
    mixed_chunks = []
    for cidx in range(n_blk):
        rows = slice(cidx * ATT_BLOCK, (cidx + 1) * ATT_BLOCK)
        groups = [_dot(w_sp[g], vn[rows, g * SG_GROUP_DIM:(g + 1) * SG_GROUP_DIM])
                  for g in range(N_SG_GROUPS)]
        mixed_chunks.append(jnp.concatenate(groups, axis=1) + sgb_ref[...])
    y_sg = u * jnp.concatenate(mixed_chunks, axis=0)
    branch_sg = _dot(_bf16(y_sg), wbs_ref[...])

    merged = (_sigmoid(proj(_GA0, D_MODEL)) * branch_att
              + _sigmoid(proj(_GS0, D_MODEL)) * branch_sg)
    y = _dot(_bf16(merged), wout_ref[...]) + bout_ref[...]
    x1 = x + gate1 * _rms(y, gpost_ref[...])
    x1_ref[0] = x1

    h2 = _rms(x1, gffn_ref[...]) * (1.0 + scale2) + shift2
    for s in range(ROW_TILES):
        h2_ref[pl.ds(s, tm, stride=ROW_TILES), :] = h2[:, s * LANES:(s + 1) * LANES]
    logits = _dot(_bf16(h2), wr_ref[...]) + br_ref[...]
    lane = lax.broadcasted_iota(jnp.int32, (tm, LANES), 1).astype(jnp.float32)
    top_val, top_idx = [], []
    selected = jnp.zeros((tm, LANES), jnp.float32)
    for _ in range(TOP_K):
        mval = jnp.max(logits, axis=-1, keepdims=True)
        midx = jnp.min(jnp.where(logits == mval, lane, float(LANES)), axis=-1, keepdims=True)
        hit = lane == midx
        selected = jnp.where(hit, 1.0, selected)
        logits = jnp.where(hit, _NEG_INF, logits)
        top_val.append(mval)
        top_idx.append(midx)
    expv = [jnp.exp(tv - top_val[0]) for tv in top_val]
    denom = expv[0] + expv[1] + expv[2] + expv[3]
    strict_lower = _bf16(jnp.where(
        lax.broadcasted_iota(jnp.int32, (ATT_BLOCK, ATT_BLOCK), 0)
        > lax.broadcasted_iota(jnp.int32, (ATT_BLOCK, ATT_BLOCK), 1), 1.0, 0.0))
    total = carry_ref[0:1, :]
    before_blocks = []
    for i in range(n_blk):
        sel_blk = selected[i * ATT_BLOCK:(i + 1) * ATT_BLOCK]
        before_blocks.append(_dot(strict_lower, _bf16(sel_blk)) + total)
        total = total + jnp.sum(sel_blk, axis=0, keepdims=True)
    before = jnp.concatenate(before_blocks, axis=0)
    route_i = jnp.zeros((tm, LANES), jnp.float32)
    route_g = jnp.zeros((tm, LANES), jnp.float32)
    for kk in range(TOP_K):
        rank = jnp.sum(jnp.where(lane == top_idx[kk], before, 0.0), axis=-1, keepdims=True)
        route_i = jnp.where(lane == float(kk), top_idx[kk], route_i)
        route_i = jnp.where(lane == float(TOP_K + kk), rank, route_i)
        route_g = jnp.where(lane == float(kk), expv[kk] / denom, route_g)
    ri_ref[...] = route_i.T[0:2 * TOP_K, :].astype(jnp.int32)
    rg_ref[...] = route_g
    carry_ref[...] = jnp.broadcast_to(total, carry_ref.shape)
    cnt_ref[...] = jnp.broadcast_to(total, cnt_ref.shape)


def _mixer(x, mod, p):
    tm = MIX_TILE
    n_j = SEQ // tm
    const2 = lambda shape: pl.BlockSpec(shape, lambda b, j, s: (0, 0), pipeline_mode=pl.Buffered(1))
    const3 = lambda shape: pl.BlockSpec(shape, lambda b, j, s: (0, 0, 0), pipeline_mode=pl.Buffered(1))
    tile_rows = lambda b, j, s: (b * n_j + j, 0)
    in_specs = [
        pl.BlockSpec((1, tm, D_MODEL), lambda b, j, s: (b, j, 0)),
        pl.BlockSpec((1, 1, 6 * D_MODEL), lambda b, j, s: (b, 0, 0)),
        const2((1, D_MODEL)), const2((1, D_MODEL)),
        const2((D_MODEL, IN_WIDTH)), const2((1, IN_WIDTH)),
        const3((N_KV_HEADS, Q_PER_KV * ATT_BLOCK, 2 * ATT_BLOCK)),
        const2((1, SG_WIDTH)), const2((1, SG_WIDTH)),
        const3((N_SG_GROUPS, ATT_BLOCK, ATT_BLOCK)), const2((ATT_BLOCK, SG_WIDTH)),
        const2((ATTN_WIDTH, D_MODEL)), const2((SG_WIDTH, D_MODEL)),
        const2((D_MODEL, D_MODEL)), const2((1, D_MODEL)),
        const2((1, D_MODEL)),
        const2((D_MODEL, LANES)), const2((1, LANES)),
    ]
    out_specs = [
        pl.BlockSpec((1, tm, D_MODEL), lambda b, j, s: (b, j, 0)),
        pl.BlockSpec((tm * ROW_TILES, LANES), tile_rows),
        pl.BlockSpec((2 * TOP_K, tm), lambda b, j, s: (0, b * n_j + j)),
        pl.BlockSpec((tm, LANES), tile_rows),
        pl.BlockSpec((SUBLANES, LANES), lambda b, j, s: (0, 0)),
    ]
    out_shape = [
        jax.ShapeDtypeStruct((BATCH, SEQ, D_MODEL), jnp.float32),
        jax.ShapeDtypeStruct((N_TOKENS * ROW_TILES, LANES), jnp.float32),
        jax.ShapeDtypeStruct((2 * TOP_K, N_TOKENS), jnp.int32),
        jax.ShapeDtypeStruct((N_TOKENS, LANES), jnp.float32),
        jax.ShapeDtypeStruct((SUBLANES, LANES), jnp.float32),
    ]
    return pl.pallas_call(
        _mixer_kernel,
        grid_spec=pltpu.PrefetchScalarGridSpec(
            num_scalar_prefetch=1,
            grid=(BATCH, n_j),
            in_specs=in_specs,
            out_specs=out_specs,
            scratch_shapes=[
                pltpu.VMEM((ATT_BLOCK, KV_WIDTH), jnp.float32),
                pltpu.VMEM((ATT_BLOCK, KV_WIDTH), jnp.float32),
                pltpu.VMEM((SUBLANES, LANES), jnp.float32),
            ]),
        out_shape=out_shape,
        compiler_params=pltpu.CompilerParams(
            dimension_semantics=("arbitrary", "arbitrary"), vmem_limit_bytes=VMEM_LIMIT_BYTES),
        name="mixer_router",
    )(p["sinks"], x, mod.reshape(BATCH, 1, 6 * D_MODEL), p["g_pre_mix"], p["g_post_mix"],
      p["w_in"], p["b_in"], _attention_bias(), p["sg_ln_g"], p["sg_ln_b"], p["sg_w"], p["sg_b"],
      p["w_br_attn"], p["w_br_sg"], p["w_out"], p["b_out"], p["g_pre_ffn"],
      p["w_router"], p["b_router"])


def _row_copy(src, src_row, dst, dst_row, sem):
    return pltpu.make_async_copy(
        src.at[pl.ds(pl.multiple_of(src_row * ROW_TILES, ROW_TILES), ROW_TILES)],
        dst.at[pl.ds(pl.multiple_of(dst_row * ROW_TILES, ROW_TILES), ROW_TILES)], sem)


def _wait_rows(src, dst, n_rows, sem):
    pltpu.make_async_copy(src.at[pl.ds(0, n_rows * ROW_TILES)],
                          dst.at[pl.ds(0, n_rows * ROW_TILES)], sem).wait()


def _tile_kmajor(slot, tile):
    n_tiles = slot.shape[1] // tile
    return slot.reshape(TOP_K, n_tiles, tile).transpose(1, 0, 2).reshape(n_tiles, 1, TOP_K * tile)


def _rows_to_matrix(rows_ref, first_row, n_rows):
    return jnp.concatenate(
        [rows_ref[pl.ds(first_row * ROW_TILES + s, n_rows, stride=ROW_TILES), :]
         for s in range(ROW_TILES)], axis=1)


def _dispatch_kernel(slot_ref, pad_ref, h2_ref, xs_hbm, sem):
    def issue(g, carry):
        for r in range(DMA_UNROLL):
            tok = g * (DMA_UNROLL // TOP_K) + r // TOP_K
            _row_copy(h2_ref, tok, xs_hbm, slot_ref[0, 0, (r % TOP_K) * DSP_TILE + tok],
                      sem).start(priority=r % 2)
        return carry

    lax.fori_loop(0, DSP_TILE * TOP_K // DMA_UNROLL, issue, 0)

    def issue_pad(g, carry):
        for r in range(DMA_UNROLL):
            _row_copy(h2_ref, 0, xs_hbm, pad_ref[0, 0, g * DMA_UNROLL + r], sem).start(priority=r % 2)
        return carry

    lax.fori_loop(0, DSP_PADS // DMA_UNROLL, issue_pad, 0)
    n_rows = (DSP_TILE * TOP_K + DSP_PADS) * ROW_TILES
    pltpu.make_async_copy(xs_hbm.at[pl.ds(0, n_rows)], xs_hbm.at[pl.ds(n_rows, n_rows)], sem).wait()


def _dispatch(slot, pad_slots, h2_rows):
    n_steps = N_TOKENS // DSP_TILE
    return pl.pallas_call(
        _dispatch_kernel,
        grid=(n_steps,),
        in_specs=[
            pl.BlockSpec((1, 1, TOP_K * DSP_TILE), lambda i: (i, 0, 0), memory_space=pltpu.SMEM),
            pl.BlockSpec((1, 1, DSP_PADS), lambda i: (i, 0, 0), memory_space=pltpu.SMEM),
            pl.BlockSpec((DSP_TILE * ROW_TILES, LANES), lambda i: (i, 0)),
        ],
        out_specs=pl.BlockSpec(memory_space=pl.ANY),
        out_shape=jax.ShapeDtypeStruct((N_SLOTS * ROW_TILES, LANES), jnp.float32),
        scratch_shapes=[pltpu.SemaphoreType.DMA],
        compiler_params=pltpu.CompilerParams(dimension_semantics=("arbitrary",)),
        name="dispatch",
    )(_tile_kmajor(slot, DSP_TILE), pad_slots.reshape(n_steps, 1, DSP_PADS), h2_rows)


def _expert_kernel(be_ref, nused_ref, xs_ref, w1_ref, b1_ref, w2_ref, b2_ref, ys_ref, w1b_ref, w2b_ref):
    i = pl.program_id(0)
    used = i < nused_ref[0]

    @pl.when(used & ((i == 0) | (be_ref[i] != be_ref[jnp.maximum(i - 1, 0)])))
    def _():
        w1b_ref[...] = _bf16(w1_ref[0])
        w2b_ref[...] = _bf16(w2_ref[0])

    @pl.when(used)
    def _():
        xb = _bf16(_rows_to_matrix(xs_ref, 0, MOE_BLOCK))
        hb = _dot(xb, w1b_ref[...]) + b1_ref[0]
        g = jnp.minimum(hb[:, :D_EXPERT], SWIGLU_LIMIT)
        lin = jnp.clip(hb[:, D_EXPERT:], -SWIGLU_LIMIT, SWIGLU_LIMIT)
        act = g * _sigmoid(SWIGLU_ALPHA * g) * (lin + 1.0)
        y = _dot(_bf16(act), w2b_ref[...]) + b2_ref[0]
        for s in range(ROW_TILES):
            ys_ref[pl.ds(s, MOE_BLOCK, stride=ROW_TILES), :] = y[:, s * LANES:(s + 1) * LANES]

    @pl.when(jnp.logical_not(used))
    def _():
        ys_ref[...] = jnp.zeros_like(ys_ref)


def _experts(block_e, n_used, xs_rows, w1, b1, w2, b2):
    return pl.pallas_call(
        _expert_kernel,
        grid_spec=pltpu.PrefetchScalarGridSpec(
            num_scalar_prefetch=2,
            grid=(N_MOE_BLOCKS,),
            in_specs=[
                pl.BlockSpec((MOE_BLOCK * ROW_TILES, LANES), lambda i, be, nu: (i, 0)),
                pl.BlockSpec((1, D_MODEL, 2 * D_EXPERT), lambda i, be, nu: (be[i], 0, 0)),
                pl.BlockSpec((1, 1, 2 * D_EXPERT), lambda i, be, nu: (be[i], 0, 0)),
                pl.BlockSpec((1, D_EXPERT, D_MODEL), lambda i, be, nu: (be[i], 0, 0)),
                pl.BlockSpec((1, 1, D_MODEL), lambda i, be, nu: (be[i], 0, 0)),
            ],
            out_specs=pl.BlockSpec((MOE_BLOCK * ROW_TILES, LANES), lambda i, be, nu: (i, 0)),
            scratch_shapes=[
                pltpu.VMEM((D_MODEL, 2 * D_EXPERT), jnp.bfloat16),
                pltpu.VMEM((D_EXPERT, D_MODEL), jnp.bfloat16),
            ],
        ),
        out_shape=jax.ShapeDtypeStruct((N_SLOTS * ROW_TILES, LANES), jnp.float32),
        compiler_params=pltpu.CompilerParams(
            dimension_semantics=("arbitrary",), vmem_limit_bytes=VMEM_LIMIT_BYTES),
        name="experts",
    )(block_e, n_used, xs_rows,
      w1, b1.reshape(N_EXPERTS, 1, 2 * D_EXPERT), w2, b2.reshape(N_EXPERTS, 1, D_MODEL))


def _combine_kernel(slot0_ref, slot_next_ref, ys_hbm, x1_ref, rg_ref, mod_ref, gpost_ref, o_ref,
                    buf_ref, sems):
    tm = CMB_TILE
    n_rows = TOP_K * tm
    i = pl.program_id(0)
    n_steps = pl.num_programs(0)

    def gather(slot_ref, half):
        def issue(g, carry):
            for r in range(DMA_UNROLL):
                tok = g * (DMA_UNROLL // TOP_K) + r // TOP_K
                _row_copy(ys_hbm, slot_ref[0, 0, (r % TOP_K) * tm + tok], buf_ref.at[half],
                          (r % TOP_K) * tm + tok, sems.at[half]).start(priority=r % 2)
            return carry
        lax.fori_loop(0, n_rows // DMA_UNROLL, issue, 0)

    @pl.when(i == 0)
    def _():
        gather(slot0_ref, 0)

    @pl.when(i + 1 < n_steps)
    def _():
        gather(slot_next_ref, (i + 1) % 2)

    cur = i % 2
    _wait_rows(ys_hbm, buf_ref.at[cur], n_rows, sems.at[cur])
    gates = rg_ref[...]
    rows = buf_ref.at[cur]
    y = jnp.zeros((tm, D_MODEL), jnp.float32)
    for kk in range(TOP_K):
        y = y + _rows_to_matrix(rows, kk * tm, tm) * gates[:, kk:kk + 1]
    gate2 = mod_ref[0][:, 5 * D_MODEL:6 * D_MODEL]
    o_ref[...] = x1_ref[...] + gate2 * _rms(y, gpost_ref[...])


def _combine(slot, ys_rows, x1, route_g, mod, g_post_ffn):
    tm = CMB_TILE
    slot_tiles = _tile_kmajor(slot, tm)
    n_tiles = N_TOKENS // tm
    tiles_per_seq = SEQ // tm
    return pl.pallas_call(
        _combine_kernel,
        grid=(n_tiles,),
        in_specs=[
            pl.BlockSpec((1, 1, TOP_K * tm), lambda i: (0, 0, 0), memory_space=pltpu.SMEM),
            pl.BlockSpec((1, 1, TOP_K * tm), lambda i: (jnp.minimum(i + 1, n_tiles - 1), 0, 0),
                         memory_space=pltpu.SMEM),
            pl.BlockSpec(memory_space=pl.ANY),
            pl.BlockSpec((tm, D_MODEL), lambda i: (i, 0)),
            pl.BlockSpec((tm, LANES), lambda i: (i, 0)),
            pl.BlockSpec((1, 1, 6 * D_MODEL), lambda i: (i // tiles_per_seq, 0, 0)),
            pl.BlockSpec((1, D_MODEL), lambda i: (0, 0)),
        ],
        out_specs=pl.BlockSpec((tm, D_MODEL), lambda i: (i, 0)),
        out_shape=jax.ShapeDtypeStruct((N_TOKENS, D_MODEL), jnp.float32),
        scratch_shapes=[
            pltpu.VMEM((2, TOP_K * tm * ROW_TILES, LANES), jnp.float32),
            pltpu.SemaphoreType.DMA((2,)),
        ],
        compiler_params=pltpu.CompilerParams(
            dimension_semantics=("arbitrary",), vmem_limit_bytes=VMEM_LIMIT_BYTES),
        name="combine",
    )(slot_tiles, slot_tiles, ys_rows, x1, route_g, mod.reshape(BATCH, 1, 6 * D_MODEL),
      g_post_ffn)


def kernel(x, c, w_mod, b_mod, g_pre_mix, g_post_mix, w_in, b_in, attn_sinks, sg_ln_g, sg_ln_b,
           sg_w, sg_b, w_br_attn, w_br_sg, w_out, b_out, g_pre_ffn, g_post_ffn, w_router,
           b_router, w_mlp1, b_mlp1, w_mlp2, b_mlp2):
    assert x.shape == (BATCH, SEQ, D_MODEL) and w_mod.shape[0] == 1, "single-layer shapes only"
    row = lambda a: a[0].reshape(1, -1)
    bf = lambda a: a[0].astype(jnp.bfloat16)

    mod = _modulation(c, w_mod[0], b_mod[0])

    params = dict(
        sinks=attn_sinks[0], g_pre_mix=row(g_pre_mix), g_post_mix=row(g_post_mix),
        w_in=bf(w_in), b_in=row(b_in), sg_ln_g=row(sg_ln_g), sg_ln_b=row(sg_ln_b), sg_w=sg_w[0],
        sg_b=jnp.repeat(sg_b[0].T, SG_GROUP_DIM, axis=1),
        w_br_attn=bf(w_br_attn), w_br_sg=bf(w_br_sg), w_out=bf(w_out), b_out=row(b_out),
        g_pre_ffn=row(g_pre_ffn),
        w_router=jnp.pad(bf(w_router), ((0, 0), (0, LANES - N_EXPERTS))),
        b_router=jnp.pad(row(b_router), ((0, 0), (0, LANES - N_EXPERTS)), constant_values=_NEG_INF),
    )
    x1, h2_rows, route_i, route_g, counts = _mixer(x, mod, params)

    counts = counts[0, :N_EXPERTS].astype(jnp.int32)
    padded = (counts + MOE_BLOCK - 1) // MOE_BLOCK * MOE_BLOCK
    pends = jnp.cumsum(padded)
    pstarts = pends - padded
    block_start = jnp.arange(N_MOE_BLOCKS, dtype=jnp.int32) * MOE_BLOCK
    block_e = jnp.minimum(jnp.sum((pends[None, :] <= block_start[:, None]).astype(jnp.int32), axis=1),
                          N_EXPERTS - 1)
    expert_ids = jnp.arange(N_EXPERTS, dtype=jnp.int32)
    top_idx = route_i[:TOP_K]
    slot = (jnp.sum(jnp.where(top_idx[:, :, None] == expert_ids, pstarts, 0), axis=-1)
            + route_i[TOP_K:2 * TOP_K])
    free_start = jnp.concatenate([pstarts + counts, pends[-1:]])
    free_len = jnp.concatenate([padded - counts, N_SLOTS - pends[-1:]])
    free_end = jnp.cumsum(free_len)
    q = jnp.arange(N_PAD_SLOTS, dtype=jnp.int32)
    region = jnp.sum((free_end[None, :] <= q[:, None]).astype(jnp.int32), axis=1)
    region_hit = region[:, None] == jnp.arange(N_EXPERTS + 1, dtype=jnp.int32)
    pad_slots = q + jnp.sum(jnp.where(region_hit, free_start - (free_end - free_len), 0), axis=1)

    xs_rows = _dispatch(slot, pad_slots, h2_rows)
    n_used = (pends[-1:] // MOE_BLOCK).astype(jnp.int32)
    ys_rows = _experts(block_e, n_used, xs_rows, w_mlp1[0], b_mlp1[0], w_mlp2[0], b_mlp2[0])

    out = _combine(slot, ys_rows, x1.reshape(N_TOKENS, D_MODEL), route_g, mod, row(g_post_ffn))
    return out.reshape(BATCH, SEQ, D_MODEL)
```

```python
import jax
import jax.numpy as jnp
import numpy as np
from jax import lax
from jax.experimental import pallas as pl
from jax.experimental.pallas import tpu as pltpu

D_MODEL = 1024
BATCH = 16
SEQ = 2048
N_TOKENS = BATCH * SEQ
HEAD_DIM = 64
N_Q_HEADS = 8
N_KV_HEADS = 2
Q_PER_KV = N_Q_HEADS // N_KV_HEADS
ATTN_WIDTH = N_Q_HEADS * HEAD_DIM
KV_WIDTH = N_KV_HEADS * HEAD_DIM
ATT_BLOCK = 128
N_SG_GROUPS = 8
SG_GROUP_DIM = 64
SG_WIDTH = N_SG_GROUPS * SG_GROUP_DIM
N_EXPERTS = 32
TOP_K = 4
D_EXPERT = D_MODEL
SWIGLU_LIMIT = 7.0
SWIGLU_ALPHA = 1.702
MOE_BLOCK = 512
NORM_EPS = 1e-6
N_ASSIGN = N_TOKENS * TOP_K
N_MOE_BLOCKS = N_ASSIGN // MOE_BLOCK + N_EXPERTS
N_SLOTS = N_MOE_BLOCKS * MOE_BLOCK

_Q0 = 0
_K0 = _Q0 + ATTN_WIDTH
_V0 = _K0 + KV_WIDTH
_U0 = _V0 + KV_WIDTH
_SV0 = _U0 + SG_WIDTH
_GA0 = _SV0 + SG_WIDTH
_GS0 = _GA0 + D_MODEL
IN_WIDTH = _GS0 + D_MODEL

LANES = 128
SUBLANES = 8
ROW_TILES = D_MODEL // LANES
assert ROW_TILES == SUBLANES
VMEM_LIMIT_BYTES = 56 * 1024 * 1024

MIX_TILE = 1024
CMB_TILE = 512
DSP_TILE = 256
DMA_UNROLL = 8
N_PAD_SLOTS = N_SLOTS - N_ASSIGN
DSP_PADS = N_PAD_SLOTS // (N_TOKENS // DSP_TILE)
assert DMA_UNROLL % TOP_K == 0 and DSP_PADS % DMA_UNROLL == 0

_NEG_INF = float("-inf")


def _rms(x, g):
    return x * lax.rsqrt(jnp.mean(x * x, axis=-1, keepdims=True) + NORM_EPS) * g


def _bf16(x):
    return x.astype(jnp.bfloat16)


def _sigmoid(x):
    return 0.5 * jnp.tanh(0.5 * x) + 0.5


def _dot(a, b):
    return jnp.dot(a, b, preferred_element_type=jnp.float32)


def _mod_kernel(c_ref, w_ref, b_ref, o_ref):
    c = c_ref[...]
    c_act = c * jax.nn.sigmoid(c)
    o_ref[...] = _dot(_bf16(c_act), _bf16(w_ref[...])) + b_ref[...]


def _modulation(c, w_mod, b_mod):
    n_out = w_mod.shape[1]
    tn = D_MODEL
    return pl.pallas_call(
        _mod_kernel,
        grid=(n_out // tn,),
        in_specs=[
            pl.BlockSpec((BATCH, D_MODEL), lambda n: (0, 0)),
            pl.BlockSpec((D_MODEL, tn), lambda n: (0, n)),
            pl.BlockSpec((1, tn), lambda n: (0, n)),
        ],
        out_specs=pl.BlockSpec((BATCH, tn), lambda n: (0, n)),
        out_shape=jax.ShapeDtypeStruct((BATCH, n_out), jnp.float32),
        compiler_params=pltpu.CompilerParams(dimension_semantics=("arbitrary",)),
        name="modulation",
    )(c, w_mod, b_mod.reshape(1, n_out))


def _attention_bias():
    slopes = 2.0 ** (-8.0 * np.arange(1, N_Q_HEADS + 1) / N_Q_HEADS)
    qi = np.arange(ATT_BLOCK)[:, None]
    kj = np.arange(2 * ATT_BLOCK)[None, :]
    dist = qi + ATT_BLOCK - kj
    in_win = (dist >= 0) & (dist < ATT_BLOCK)
    bias = np.where(in_win[None], -slopes[:, None, None] * dist[None].astype(np.float64), -np.inf)
    bias = bias.reshape(N_KV_HEADS, Q_PER_KV * ATT_BLOCK, 2 * ATT_BLOCK)
    return jnp.asarray(bias, dtype=jnp.float32)


def _mixer_kernel(sinks_ref, x_ref, mod_ref, gpre_ref, gpost_ref, win_ref, bin_ref, abias_ref,
                  lng_ref, lnb_ref, sgw_ref, sgb_ref, wba_ref, wbs_ref, wout_ref, bout_ref,
                  gffn_ref, wr_ref, br_ref,
                  x1_ref, h2_ref, ri_ref, rg_ref, cnt_ref,
                  kprev_ref, vprev_ref, carry_ref):
    b = pl.program_id(0)
    j = pl.program_id(1)
    tm = MIX_TILE
    n_blk = tm // ATT_BLOCK

    @pl.when((b == 0) & (j == 0))
    def _():
        carry_ref[...] = jnp.zeros_like(carry_ref)

    @pl.when(j == 0)
    def _():
        kprev_ref[...] = jnp.zeros_like(kprev_ref)
        vprev_ref[...] = jnp.zeros_like(vprev_ref)

    x = x_ref[0]
    mod = mod_ref[0]
    shift1, scale1, gate1, shift2, scale2, gate2 = (
        mod[:, i * D_MODEL:(i + 1) * D_MODEL] for i in range(6))

    h = _bf16(_rms(x, gpre_ref[...]) * (1.0 + scale1) + shift1)

    def proj(lo, width):
        return _dot(h, win_ref[:, lo:lo + width]) + bin_ref[:, lo:lo + width]

    q = proj(_Q0, ATTN_WIDTH)
    kv = proj(_K0, 2 * KV_WIDTH)
    k = kv[:, :KV_WIDTH]
    v = kv[:, KV_WIDTH:]
    row_group = lax.broadcasted_iota(jnp.int32, (Q_PER_KV * ATT_BLOCK, 1), 0) // ATT_BLOCK
    col = lax.broadcasted_iota(jnp.int32, (Q_PER_KV * ATT_BLOCK, 2 * ATT_BLOCK), 1)
    y_att_blocks = []
    for i in range(n_blk):
        rows = slice(i * ATT_BLOCK, (i + 1) * ATT_BLOCK)
        if i == 0:
            k_prev, v_prev = kprev_ref[...], vprev_ref[...]
        else:
            prev = slice((i - 1) * ATT_BLOCK, i * ATT_BLOCK)
            k_prev, v_prev = k[prev], v[prev]
        k_cat = _bf16(jnp.concatenate([k_prev, k[rows]], axis=0))
        v_cat = _bf16(jnp.concatenate([v_prev, v[rows]], axis=0))
        q_blk = _bf16(q[rows])
        heads = [None] * N_Q_HEADS
        for hk in range(N_KV_HEADS):
            kv_cols = slice(hk * HEAD_DIM, (hk + 1) * HEAD_DIM)
            q_stack = jnp.concatenate(
                [q_blk[:, (hk * Q_PER_KV + g) * HEAD_DIM:(hk * Q_PER_KV + g + 1) * HEAD_DIM]
                 for g in range(Q_PER_KV)], axis=0)
            scores = lax.dot_general(q_stack, k_cat[:, kv_cols], (((1,), (1,)), ((), ())),
                                     preferred_element_type=jnp.float32)
            logits = scores * (HEAD_DIM ** -0.5) + abias_ref[hk]
            if i == 0:
                n_masked = jnp.where(j == 0, ATT_BLOCK, 0)
                logits = jnp.where(col < n_masked, _NEG_INF, logits)
            sink = jnp.zeros((Q_PER_KV * ATT_BLOCK, 1), jnp.float32)
            for g in range(Q_PER_KV):
                sink = jnp.where(row_group == g, sinks_ref[hk * Q_PER_KV + g], sink)
            m = jnp.maximum(jnp.max(logits, axis=-1, keepdims=True), sink)
            p = jnp.exp(logits - m)
            probs = p / (jnp.sum(p, axis=-1, keepdims=True) + jnp.exp(sink - m))
            out = _dot(_bf16(probs), v_cat[:, kv_cols])
            for g in range(Q_PER_KV):
                heads[hk * Q_PER_KV + g] = out[g * ATT_BLOCK:(g + 1) * ATT_BLOCK]
        y_att_blocks.append(jnp.concatenate(heads, axis=1))
    kprev_ref[...] = k[tm - ATT_BLOCK:]
    vprev_ref[...] = v[tm - ATT_BLOCK:]
    y_att = jnp.concatenate(y_att_blocks, axis=0)
    branch_att = _dot(_bf16(y_att), wba_ref[...])

    u = jax.nn.gelu(proj(_U0, SG_WIDTH))
    sv = jax.nn.gelu(proj(_SV0, SG_WIDTH))
    mu = jnp.mean(sv, axis=-1, keepdims=True)
    var = jnp.mean(jnp.square(sv - mu), axis=-1, keepdims=True)
    vn = _bf16((sv - mu) * lax.rsqrt(var + NORM_EPS) * lng_ref[...] + lnb_ref[...])
    causal = (lax.broadcasted_iota(jnp.int32, (ATT_BLOCK, ATT_BLOCK), 0)
              >= lax.broadcasted_iota(jnp.int32, (ATT_BLOCK, ATT_BLOCK), 1))
    w_sp = [_bf16(jnp.where(causal, sgw_ref[g], 0.0)) for g in range(N_SG_GROUPS)]
    per_group = []
    for g in range(N_SG_GROUPS):
        cols = slice(g * SG_GROUP_DIM, (g + 1) * SG_GROUP_DIM)
        rhs = jnp.concatenate([vn[c * ATT_BLOCK:(c + 1) * ATT_BLOCK, cols] for c in range(n_blk)],
                              axis=1)
        per_group.append(_dot(w_sp[g], rhs))
    mixed_chunks = []
    for cidx in range(n_blk):
        cols = slice(cidx * SG_GROUP_DIM, (cidx + 1) * SG_GROUP_DIM)
        mixed_chunks.append(jnp.concatenate([pg[:, cols] for pg in per_group], axis=1)
                            + sgb_ref[...])
    y_sg = u * jnp.concatenate(mixed_chunks, axis=0)
    branch_sg = _dot(_bf16(y_sg), wbs_ref[...])

    merged = (_sigmoid(proj(_GA0, D_MODEL)) * branch_att
              + _sigmoid(proj(_GS0, D_MODEL)) * branch_sg)
    y = _dot(_bf16(merged), wout_ref[...]) + bout_ref[...]
    x1 = x + gate1 * _rms(y, gpost_ref[...])
    x1_ref[0] = x1

    h2 = _rms(x1, gffn_ref[...]) * (1.0 + scale2) + shift2
    for s in range(ROW_TILES):
        h2_ref[pl.ds(s, tm, stride=ROW_TILES), :] = h2[:, s * LANES:(s + 1) * LANES]
    logits = _dot(_bf16(h2), wr_ref[...]) + br_ref[...]
    lane = lax.broadcasted_iota(jnp.int32, (tm, LANES), 1).astype(jnp.float32)
    top_val, top_idx = [], []
    selected = jnp.zeros((tm, LANES), jnp.float32)
    for _ in range(TOP_K):
        mval = jnp.max(logits, axis=-1, keepdims=True)
        midx = jnp.min(jnp.where(logits == mval, lane, float(LANES)), axis=-1, keepdims=True)
        hit = lane == midx
        selected = jnp.where(hit, 1.0, selected)
        logits = jnp.where(hit, _NEG_INF, logits)
        top_val.append(mval)
        top_idx.append(midx)
    expv = [jnp.exp(tv - top_val[0]) for tv in top_val]
    denom = expv[0] + expv[1] + expv[2] + expv[3]
    strict_lower = _bf16(jnp.where(
        lax.broadcasted_iota(jnp.int32, (ATT_BLOCK, ATT_BLOCK), 0)
        > lax.broadcasted_iota(jnp.int32, (ATT_BLOCK, ATT_BLOCK), 1), 1.0, 0.0))
    total = carry_ref[0:1, :]
    before_blocks = []
    for i in range(n_blk):
        sel_blk = selected[i * ATT_BLOCK:(i + 1) * ATT_BLOCK]
        before_blocks.append(_dot(strict_lower, _bf16(sel_blk)) + total)
        total = total + jnp.sum(sel_blk, axis=0, keepdims=True)
    before = jnp.concatenate(before_blocks, axis=0)
    route_i = jnp.zeros((tm, LANES), jnp.float32)
    route_g = jnp.zeros((tm, LANES), jnp.float32)
    for kk in range(TOP_K):
        rank = jnp.sum(jnp.where(lane == top_idx[kk], before, 0.0), axis=-1, keepdims=True)
        route_i = jnp.where(lane == float(kk), top_idx[kk], route_i)
        route_i = jnp.where(lane == float(TOP_K + kk), rank, route_i)
        route_g = jnp.where(lane == float(kk), expv[kk] / denom, route_g)
    ri_ref[...] = route_i.T[0:2 * TOP_K, :].astype(jnp.int32)
    rg_ref[...] = route_g
    carry_ref[...] = jnp.broadcast_to(total, carry_ref.shape)
    cnt_ref[...] = jnp.broadcast_to(total, cnt_ref.shape)


def _mixer(x, mod, p):
    tm = MIX_TILE
    n_j = SEQ // tm
    const2 = lambda shape: pl.BlockSpec(shape, lambda b, j, s: (0, 0), pipeline_mode=pl.Buffered(1))
    const3 = lambda shape: pl.BlockSpec(shape, lambda b, j, s: (0, 0, 0), pipeline_mode=pl.Buffered(1))
    tile_rows = lambda b, j, s: (b * n_j + j, 0)
    in_specs = [
        pl.BlockSpec((1, tm, D_MODEL), lambda b, j, s: (b, j, 0)),
        pl.BlockSpec((1, 1, 6 * D_MODEL), lambda b, j, s: (b, 0, 0)),
        const2((1, D_MODEL)), const2((1, D_MODEL)),
        const2((D_MODEL, IN_WIDTH)), const2((1, IN_WIDTH)),
        const3((N_KV_HEADS, Q_PER_KV * ATT_BLOCK, 2 * ATT_BLOCK)),
        const2((1, SG_WIDTH)), const2((1, SG_WIDTH)),
        const3((N_SG_GROUPS, ATT_BLOCK, ATT_BLOCK)), const2((ATT_BLOCK, SG_WIDTH)),
        const2((ATTN_WIDTH, D_MODEL)), const2((SG_WIDTH, D_MODEL)),
        const2((D_MODEL, D_MODEL)), const2((1, D_MODEL)),
        const2((1, D_MODEL)),
        const2((D_MODEL, LANES)), const2((1, LANES)),
    ]
    out_specs = [
        pl.BlockSpec((1, tm, D_MODEL), lambda b, j, s: (b, j, 0)),
        pl.BlockSpec((tm * ROW_TILES, LANES), tile_rows),
        pl.BlockSpec((2 * TOP_K, tm), lambda b, j, s: (0, b * n_j + j)),
        pl.BlockSpec((tm, LANES), tile_rows),
        pl.BlockSpec((SUBLANES, LANES), lambda b, j, s: (0, 0)),
    ]
    out_shape = [
        jax.ShapeDtypeStruct((BATCH, SEQ, D_MODEL), jnp.float32),
        jax.ShapeDtypeStruct((N_TOKENS * ROW_TILES, LANES), jnp.float32),
        jax.ShapeDtypeStruct((2 * TOP_K, N_TOKENS), jnp.int32),
        jax.ShapeDtypeStruct((N_TOKENS, LANES), jnp.float32),
        jax.ShapeDtypeStruct((SUBLANES, LANES), jnp.float32),
    ]
    return pl.pallas_call(
        _mixer_kernel,
        grid_spec=pltpu.PrefetchScalarGridSpec(
            num_scalar_prefetch=1,
            grid=(BATCH, n_j),
            in_specs=in_specs,
            out_specs=out_specs,
            scratch_shapes=[
                pltpu.VMEM((ATT_BLOCK, KV_WIDTH), jnp.float32),
                pltpu.VMEM((ATT_BLOCK, KV_WIDTH), jnp.float32),
                pltpu.VMEM((SUBLANES, LANES), jnp.float32),
            ]),
        out_shape=out_shape,
        compiler_params=pltpu.CompilerParams(
            dimension_semantics=("arbitrary", "arbitrary"), vmem_limit_bytes=VMEM_LIMIT_BYTES),
        name="mixer_router",
    )(p["sinks"], x, mod.reshape(BATCH, 1, 6 * D_MODEL), p["g_pre_mix"], p["g_post_mix"],
      p["w_in"], p["b_in"], _attention_bias(), p["sg_ln_g"], p["sg_ln_b"], p["sg_w"], p["sg_b"],
      p["w_br_attn"], p["w_br_sg"], p["w_out"], p["b_out"], p["g_pre_ffn"],
      p["w_router"], p["b_router"])


def _row_copy(src, src_row, dst, dst_row, sem):
    return pltpu.make_async_copy(
        src.at[pl.ds(pl.multiple_of(src_row * ROW_TILES, ROW_TILES), ROW_TILES)],
        dst.at[pl.ds(pl.multiple_of(dst_row * ROW_TILES, ROW_TILES), ROW_TILES)], sem)


def _wait_rows(src, dst, n_rows, sem):
    pltpu.make_async_copy(src.at[pl.ds(0, n_rows * ROW_TILES)],
                          dst.at[pl.ds(0, n_rows * ROW_TILES)], sem).wait()


def _tile_kmajor(slot, tile):
    n_tiles = slot.shape[1] // tile
    return slot.reshape(TOP_K, n_tiles, tile).transpose(1, 0, 2).reshape(n_tiles, 1, TOP_K * tile)


def _rows_to_matrix(rows_ref, first_row, n_rows):
    return jnp.concatenate(
        [rows_ref[pl.ds(first_row * ROW_TILES + s, n_rows, stride=ROW_TILES), :]
         for s in range(ROW_TILES)], axis=1)


def _dispatch_kernel(slot_ref, pad_ref, h2_ref, xs_hbm, sem):
    def issue(g, carry):
        for r in range(DMA_UNROLL):
            tok = g * (DMA_UNROLL // TOP_K) + r // TOP_K
            _row_copy(h2_ref, tok, xs_hbm, slot_ref[0, 0, (r % TOP_K) * DSP_TILE + tok],
                      sem).start(priority=r % 2)
        return carry

    lax.fori_loop(0, DSP_TILE * TOP_K // DMA_UNROLL, issue, 0)

    def issue_pad(g, carry):
        for r in range(DMA_UNROLL):
            _row_copy(h2_ref, 0, xs_hbm, pad_ref[0, 0, g * DMA_UNROLL + r], sem).start(priority=r % 2)
        return carry

    lax.fori_loop(0, DSP_PADS // DMA_UNROLL, issue_pad, 0)
    n_rows = (DSP_TILE * TOP_K + DSP_PADS) * ROW_TILES
    pltpu.make_async_copy(xs_hbm.at[pl.ds(0, n_rows)], xs_hbm.at[pl.ds(n_rows, n_rows)], sem).wait()


def _dispatch(slot, pad_slots, h2_rows):
    n_steps = N_TOKENS // DSP_TILE
    return pl.pallas_call(
        _dispatch_kernel,
        grid=(n_steps,),
        in_specs=[
            pl.BlockSpec((1, 1, TOP_K * DSP_TILE), lambda i: (i, 0, 0), memory_space=pltpu.SMEM),
            pl.BlockSpec((1, 1, DSP_PADS), lambda i: (i, 0, 0), memory_space=pltpu.SMEM),
            pl.BlockSpec((DSP_TILE * ROW_TILES, LANES), lambda i: (i, 0)),
        ],
        out_specs=pl.BlockSpec(memory_space=pl.ANY),
        out_shape=jax.ShapeDtypeStruct((N_SLOTS * ROW_TILES, LANES), jnp.float32),
        scratch_shapes=[pltpu.SemaphoreType.DMA],
        compiler_params=pltpu.CompilerParams(dimension_semantics=("arbitrary",)),
        name="dispatch",
    )(_tile_kmajor(slot, DSP_TILE), pad_slots.reshape(n_steps, 1, DSP_PADS), h2_rows)


def _expert_kernel(be_ref, nused_ref, xs_ref, w1_ref, b1_ref, w2_ref, b2_ref, ys_ref, w1b_ref, w2b_ref):
    i = pl.program_id(0)
    used = i < nused_ref[0]

    @pl.when(used & ((i == 0) | (be_ref[i] != be_ref[jnp.maximum(i - 1, 0)])))
    def _():
        w1b_ref[...] = _bf16(w1_ref[0])
        w2b_ref[...] = _bf16(w2_ref[0])

    @pl.when(used)
    def _():
        xb = _bf16(_rows_to_matrix(xs_ref, 0, MOE_BLOCK))
        hb = _dot(xb, w1b_ref[...]) + b1_ref[0]
        g = jnp.minimum(hb[:, :D_EXPERT], SWIGLU_LIMIT)
        lin = jnp.clip(hb[:, D_EXPERT:], -SWIGLU_LIMIT, SWIGLU_LIMIT)
        act = g * _sigmoid(SWIGLU_ALPHA * g) * (lin + 1.0)
        y = _dot(_bf16(act), w2b_ref[...]) + b2_ref[0]
        for s in range(ROW_TILES):
            ys_ref[pl.ds(s, MOE_BLOCK, stride=ROW_TILES), :] = y[:, s * LANES:(s + 1) * LANES]

    @pl.when(jnp.logical_not(used))
    def _():
        ys_ref[...] = jnp.zeros_like(ys_ref)


def _experts(block_e, n_used, xs_rows, w1, b1, w2, b2):
    return pl.pallas_call(
        _expert_kernel,
        grid_spec=pltpu.PrefetchScalarGridSpec(
            num_scalar_prefetch=2,
            grid=(N_MOE_BLOCKS,),
            in_specs=[
                pl.BlockSpec((MOE_BLOCK * ROW_TILES, LANES), lambda i, be, nu: (i, 0)),
                pl.BlockSpec((1, D_MODEL, 2 * D_EXPERT), lambda i, be, nu: (be[i], 0, 0)),
                pl.BlockSpec((1, 1, 2 * D_EXPERT), lambda i, be, nu: (be[i], 0, 0)),
                pl.BlockSpec((1, D_EXPERT, D_MODEL), lambda i, be, nu: (be[i], 0, 0)),
                pl.BlockSpec((1, 1, D_MODEL), lambda i, be, nu: (be[i], 0, 0)),
            ],
            out_specs=pl.BlockSpec((MOE_BLOCK * ROW_TILES, LANES), lambda i, be, nu: (i, 0)),
            scratch_shapes=[
                pltpu.VMEM((D_MODEL, 2 * D_EXPERT), jnp.bfloat16),
                pltpu.VMEM((D_EXPERT, D_MODEL), jnp.bfloat16),
            ],
        ),
        out_shape=jax.ShapeDtypeStruct((N_SLOTS * ROW_TILES, LANES), jnp.float32),
        compiler_params=pltpu.CompilerParams(
            dimension_semantics=("arbitrary",), vmem_limit_bytes=VMEM_LIMIT_BYTES),
        name="experts",
    )(block_e, n_used, xs_rows,
      w1, b1.reshape(N_EXPERTS, 1, 2 * D_EXPERT), w2, b2.reshape(N_EXPERTS, 1, D_MODEL))


def _combine_kernel(slot0_ref, slot_next_ref, ys_hbm, x1_ref, rg_ref, mod_ref, gpost_ref, o_ref,
                    buf_ref, sems):
    tm = CMB_TILE
    n_rows = TOP_K * tm
    i = pl.program_id(0)
    n_steps = pl.num_programs(0)

    def gather(slot_ref, half):
        def issue(g, carry):
            for r in range(DMA_UNROLL):
                tok = g * (DMA_UNROLL // TOP_K) + r // TOP_K
                _row_copy(ys_hbm, slot_ref[0, 0, (r % TOP_K) * tm + tok], buf_ref.at[half],
                          (r % TOP_K) * tm + tok, sems.at[half]).start(priority=r % 2)
            return carry
        lax.fori_loop(0, n_rows // DMA_UNROLL, issue, 0)

    @pl.when(i == 0)
    def _():
        gather(slot0_ref, 0)

    @pl.when(i + 1 < n_steps)
    def _():
        gather(slot_next_ref, (i + 1) % 2)

    cur = i % 2
    _wait_rows(ys_hbm, buf_ref.at[cur], n_rows, sems.at[cur])
    gates = rg_ref[...]
    rows = buf_ref.at[cur]
    y = jnp.zeros((tm, D_MODEL), jnp.float32)
    for kk in range(TOP_K):
        y = y + _rows_to_matrix(rows, kk * tm, tm) * gates[:, kk:kk + 1]
    gate2 = mod_ref[0][:, 5 * D_MODEL:6 * D_MODEL]
    o_ref[...] = x1_ref[...] + gate2 * _rms(y, gpost_ref[...])


def _combine(slot, ys_rows, x1, route_g, mod, g_post_ffn):
    tm = CMB_TILE
    slot_tiles = _tile_kmajor(slot, tm)
    n_tiles = N_TOKENS // tm
    tiles_per_seq = SEQ // tm
    return pl.pallas_call(
        _combine_kernel,
        grid=(n_tiles,),
        in_specs=[
            pl.BlockSpec((1, 1, TOP_K * tm), lambda i: (0, 0, 0), memory_space=pltpu.SMEM),
            pl.BlockSpec((1, 1, TOP_K * tm), lambda i: (jnp.minimum(i + 1, n_tiles - 1), 0, 0),
                         memory_space=pltpu.SMEM),
            pl.BlockSpec(memory_space=pl.ANY),
            pl.BlockSpec((tm, D_MODEL), lambda i: (i, 0)),
            pl.BlockSpec((tm, LANES), lambda i: (i, 0)),
            pl.BlockSpec((1, 1, 6 * D_MODEL), lambda i: (i // tiles_per_seq, 0, 0)),
            pl.BlockSpec((1, D_MODEL), lambda i: (0, 0)),
        ],
        out_specs=pl.BlockSpec((tm, D_MODEL), lambda i: (i, 0)),
        out_shape=jax.ShapeDtypeStruct((N_TOKENS, D_MODEL), jnp.float32),
        scratch_shapes=[
            pltpu.VMEM((2, TOP_K * tm * ROW_TILES, LANES), jnp.float32),
            pltpu.SemaphoreType.DMA((2,)),
        ],
        compiler_params=pltpu.CompilerParams(
            dimension_semantics=("arbitrary",), vmem_limit_bytes=VMEM_LIMIT_BYTES),
        name="combine",
    )(slot_tiles, slot_tiles, ys_rows, x1, route_g, mod.reshape(BATCH, 1, 6 * D_MODEL),
      g_post_ffn)


def kernel(x, c, w_mod, b_mod, g_pre_mix, g_post_mix, w_in, b_in, attn_sinks, sg_ln_g, sg_ln_b,
           sg_w, sg_b, w_br_attn, w_br_sg, w_out, b_out, g_pre_ffn, g_post_ffn, w_router,
           b_router, w_mlp1, b_mlp1, w_mlp2, b_mlp2):
    assert x.shape == (BATCH, SEQ, D_MODEL) and w_mod.shape[0] == 1, "single-layer shapes only"
    row = lambda a: a[0].reshape(1, -1)
    bf = lambda a: a[0].astype(jnp.bfloat16)

    mod = _modulation(c, w_mod[0], b_mod[0])

    params = dict(
        sinks=attn_sinks[0], g_pre_mix=row(g_pre_mix), g_post_mix=row(g_post_mix),
        w_in=bf(w_in), b_in=row(b_in), sg_ln_g=row(sg_ln_g), sg_ln_b=row(sg_ln_b), sg_w=sg_w[0],
        sg_b=jnp.repeat(sg_b[0].T, SG_GROUP_DIM, axis=1),
        w_br_attn=bf(w_br_attn), w_br_sg=bf(w_br_sg), w_out=bf(w_out), b_out=row(b_out),
        g_pre_ffn=row(g_pre_ffn),
        w_router=jnp.pad(bf(w_router), ((0, 0), (0, LANES - N_EXPERTS))),
        b_router=jnp.pad(row(b_router), ((0, 0), (0, LANES - N_EXPERTS)), constant_values=_NEG_INF),
    )
    x1, h2_rows, route_i, route_g, counts = _mixer(x, mod, params)

    counts = counts[0, :N_EXPERTS].astype(jnp.int32)
    padded = (counts + MOE_BLOCK - 1) // MOE_BLOCK * MOE_BLOCK
    pends = jnp.cumsum(padded)
    pstarts = pends - padded
    block_start = jnp.arange(N_MOE_BLOCKS, dtype=jnp.int32) * MOE_BLOCK
    block_e = jnp.minimum(jnp.sum((pends[None, :] <= block_start[:, None]).astype(jnp.int32), axis=1),
                          N_EXPERTS - 1)
    expert_ids = jnp.arange(N_EXPERTS, dtype=jnp.int32)
    top_idx = route_i[:TOP_K]
    slot = (jnp.sum(jnp.where(top_idx[:, :, None] == expert_ids, pstarts, 0), axis=-1)
            + route_i[TOP_K:2 * TOP_K])
    free_start = jnp.concatenate([pstarts + counts, pends[-1:]])
    free_len = jnp.concatenate([padded - counts, N_SLOTS - pends[-1:]])
    free_end = jnp.cumsum(free_len)
    q = jnp.arange(N_PAD_SLOTS, dtype=jnp.int32)
    region = jnp.sum((free_end[None, :] <= q[:, None]).astype(jnp.int32), axis=1)
    region_hit = region[:, None] == jnp.arange(N_EXPERTS + 1, dtype=jnp.int32)
    pad_slots = q + jnp.sum(jnp.where(region_hit, free_start - (free_end - free_len), 0), axis=1)

    xs_rows = _dispatch(slot, pad_slots, h2_rows)
    n_used = (pends[-1:] // MOE_BLOCK).astype(jnp.int32)
    ys_rows = _experts(block_e, n_used, xs_rows, w_mlp1[0], b_mlp1[0], w_mlp2[0], b_mlp2[0])

    out = _combine(slot, ys_rows, x1.reshape(N_TOKENS, D_MODEL), route_g, mod, row(g_post_ffn))
    return out.reshape(BATCH, SEQ, D_MODEL)
```
